```python
import math
import jax
import jax.numpy as jnp
from jax import lax
import numpy as np

D_MODEL = 1024
BATCH = 8
SEQ = 2048
DEPTH = 4

N_MEM = 256
ROPE_THETA = 500000.0
EPS = 1e-6
Q_BLOCK = 128

MLA_HEADS = 8
MLA_NOPE = 64
MLA_ROPE = 32
MLA_V = 64
MLA_Q_LORA = 384
MLA_KV_LORA = 256

SGU_GROUPS = 4
SGU_GROUP_DIM = 64
SGU_WIDTH = SGU_GROUPS * SGU_GROUP_DIM
SGU_CHUNK = 128

DIFF_HEADS = 4
DIFF_QK_DIM = 32
DIFF_V_DIM = 2 * DIFF_QK_DIM
DIFF_ROT = (2 * DIFF_QK_DIM) // 4 // 2

MIX_WIDTH = MLA_HEADS * MLA_V + SGU_WIDTH + DIFF_HEADS * DIFF_V_DIM

P_CQ = MLA_Q_LORA
P_CKV = MLA_KV_LORA
P_KR = MLA_ROPE
P_SGU = 2 * SGU_WIDTH
P_DQ = DIFF_HEADS * 2 * DIFF_QK_DIM
P_DK = DIFF_HEADS * 2 * DIFF_QK_DIM
P_DV = DIFF_HEADS * DIFF_V_DIM
IN_WIDTH = P_CQ + P_CKV + P_KR + P_SGU + P_DQ + P_DK + P_DV
IN_SPLIT = (P_CQ, P_CQ + P_CKV, P_CQ + P_CKV + P_KR, P_CQ + P_CKV + P_KR + P_SGU,
            P_CQ + P_CKV + P_KR + P_SGU + P_DQ, P_CQ + P_CKV + P_KR + P_SGU + P_DQ + P_DK)

X_HEADS = 4
X_HEAD_DIM = D_MODEL // X_HEADS

D_FF = 2816
CONV_WIDTH = 3

kernel_name = "hybrid_mla_gmlp_diffattn_encoder"


def rms_norm(x, g):
    xf = x.astype(jnp.float32)
    y = xf * lax.rsqrt(jnp.mean(xf * xf, axis=-1, keepdims=True) + EPS)
    return (y * g.astype(jnp.float32)).astype(x.dtype)


def layer_norm(x, g):
    xf = x.astype(jnp.float32)
    mu = jnp.mean(xf, axis=-1, keepdims=True)
    xc = xf - mu
    y = xc * lax.rsqrt(jnp.mean(xc * xc, axis=-1, keepdims=True) + EPS)
    return (y * g.astype(jnp.float32)).astype(x.dtype)


def rope_tables(positions, rot_dim):
    inv_freq = ROPE_THETA ** (-jnp.arange(0, rot_dim, 2, dtype=jnp.float32) / rot_dim)
    ang = positions.astype(jnp.float32)[..., None] * inv_freq
    return jnp.cos(ang), jnp.sin(ang)


def apply_rope(x, cos, sin):
    bshape = cos.shape[:2] + (1,) * (x.ndim - 3) + cos.shape[-1:]
    c = cos.reshape(bshape)
    s = sin.reshape(bshape)
    xf = x.astype(jnp.float32)
    x1, x2 = jnp.split(xf, 2, axis=-1)
    return jnp.concatenate([x1 * c - x2 * s, x2 * c + x1 * s], axis=-1).astype(x.dtype)


def partial_rope(x, cos, sin, rot):
    return jnp.concatenate([apply_rope(x[..., :rot], cos, sin), x[..., rot:]], axis=-1)


def to_query_blocks(q):
    b, s = q.shape[:2]
    return jnp.moveaxis(q.reshape((b, s // Q_BLOCK, Q_BLOCK) + q.shape[2:]), 1, 0)


def from_query_blocks(o):
    nb, b, qb = o.shape[:3]
    return jnp.moveaxis(o, 0, 1).reshape((b, nb * qb) + o.shape[3:])


def blocked_attention(q, k, v, scale):
    def block(qb):
        s = jnp.einsum('bqhd,bkhd->bhqk', qb, k).astype(jnp.float32) * scale
        p = jax.nn.softmax(s, axis=-1).astype(v.dtype)
        return jnp.einsum('bhqk,bkhd->bqhd', p, v)
    return from_query_blocks(lax.map(block, to_query_blocks(q)))


def mla_group(cq, ckv, kr, g_cq, g_ckv, w_uq, w_ukv, cos, sin):
    b, s, _ = cq.shape
    q = (rms_norm(cq, g_cq) @ w_uq).reshape(b, s, MLA_HEADS, MLA_NOPE + MLA_ROPE)
    q = jnp.concatenate([q[..., :MLA_NOPE], apply_rope(q[..., MLA_NOPE:], cos, sin)], axis=-1)
    kv = (rms_norm(ckv, g_ckv) @ w_ukv).reshape(b, s, MLA_HEADS, MLA_NOPE + MLA_V)
    k_nope, v = kv[..., :MLA_NOPE], kv[..., MLA_NOPE:]
    k_rope = apply_rope(kr[:, :, None, :], cos, sin)
    k = jnp.concatenate([k_nope, jnp.broadcast_to(k_rope, (b, s, MLA_HEADS, MLA_ROPE))], axis=-1)
    o = blocked_attention(q, k, v, (MLA_NOPE + MLA_ROPE) ** -0.5)
    return o.reshape(b, s, MLA_HEADS * MLA_V)


def sgu_group(z, g_sgu, w_s, b_s):
    b, s, _ = z.shape
    nc = s // SGU_CHUNK
    z = jax.nn.gelu(z, approximate=True)
    u, v = jnp.split(z, 2, axis=-1)
    u = u.reshape(b, nc, SGU_CHUNK, SGU_GROUPS, SGU_GROUP_DIM)
    v = layer_norm(v.reshape(b, nc, SGU_CHUNK, SGU_GROUPS, SGU_GROUP_DIM), g_sgu)
    mixed = jnp.einsum('gts,bnsgc->bntgc', w_s, v) + b_s.T[:, :, None]
    return (u * mixed).reshape(b, s, SGU_WIDTH)


def diff_group(dq, dk, dv, lam, lam_init, g_sub, cos, sin):
    b, s, _ = dq.shape
    q = partial_rope(dq.reshape(b, s, DIFF_HEADS, 2, DIFF_QK_DIM), cos, sin, 2 * DIFF_ROT)
    k = partial_rope(dk.reshape(b, s, DIFF_HEADS, 2, DIFF_QK_DIM), cos, sin, 2 * DIFF_ROT)
    v = dv.reshape(b, s, DIFF_HEADS, DIFF_V_DIM)
    scale = DIFF_QK_DIM ** -0.5

    def block(qb):
        sc = jnp.einsum('bqhcd,bkhcd->bhcqk', qb, k).astype(jnp.float32) * scale
        p = jax.nn.softmax(sc, axis=-1)
        a = (p[:, :, 0] - lam * p[:, :, 1]).astype(v.dtype)
        return jnp.einsum('bhqk,bkhe->bqhe', a, v)

    o = from_query_blocks(lax.map(block, to_query_blocks(q)))
    o = rms_norm(o, g_sub) * (1.0 - lam_init)
    return o.reshape(b, s, DIFF_HEADS * DIFF_V_DIM)


def memory_cross_attention(h, mem_n, w_q, w_kv, w_o):
    b, s, _ = h.shape
    m = mem_n.shape[1]
    q = (h @ w_q).reshape(b, s, X_HEADS, X_HEAD_DIM)
    kv = (mem_n @ w_kv).reshape(b, m, 2, X_HEADS, X_HEAD_DIM)
    k, v = kv[:, :, 0], kv[:, :, 1]
    sc = jnp.einsum('bqhd,bkhd->bhqk', q, k).astype(jnp.float32) * (X_HEAD_DIM ** -0.5)
    p = jax.nn.softmax(sc, axis=-1).astype(v.dtype)
    o = jnp.einsum('bhqk,bkhd->bqhd', p, v).reshape(b, s, D_MODEL)
    return o @ w_o


def conv_ffn(h, w_up, conv_w, conv_b, w_down):
    a = h @ w_up
    a = lax.conv_general_dilated(
        a, conv_w[:, None, :], window_strides=(1,),
        padding=((CONV_WIDTH // 2, CONV_WIDTH // 2),),
        dimension_numbers=('NWC', 'WIO', 'NWC'),
        feature_group_count=2 * D_FF) + conv_b
    g, u = jnp.split(a, 2, axis=-1)
    return (jax.nn.gelu(g, approximate=True) * u) @ w_down


def setup_inputs(seed: int = 0) -> dict:
    key = jax.random.key(seed)
    ks = iter(jax.random.split(key, 40))

    def nrm(shape, scale):
        return jax.random.normal(next(ks), shape, jnp.float32) * scale

    def gain(dim_shape):
        return 1.0 + nrm((DEPTH,) + dim_shape, 0.05)

    x = nrm((BATCH, SEQ, D_MODEL), 1.0)
    mem = nrm((BATCH, N_MEM, D_MODEL), 1.0)
    offsets = jax.random.randint(next(ks), (BATCH, 1), 0, 4096, dtype=jnp.int32)
    positions = offsets + jnp.arange(SEQ, dtype=jnp.int32)[None, :]
    return {
        'x': x, 'mem': mem, 'positions': positions,
        'mix_pre_g': gain((D_MODEL,)),
        'mix_post_g': gain((D_MODEL,)),
        'w_in': nrm((DEPTH, D_MODEL, IN_WIDTH), D_MODEL ** -0.5),
        'mla_cq_g': gain((MLA_Q_LORA,)),
        'mla_ckv_g': gain((MLA_KV_LORA,)),
        'mla_w_uq': nrm((DEPTH, MLA_Q_LORA, MLA_HEADS * (MLA_NOPE + MLA_ROPE)), MLA_Q_LORA ** -0.5),
        'mla_w_ukv': nrm((DEPTH, MLA_KV_LORA, MLA_HEADS * (MLA_NOPE + MLA_V)), MLA_KV_LORA ** -0.5),
        'sgu_norm_g': gain((SGU_GROUPS, SGU_GROUP_DIM)),
        'sgu_w_s': nrm((DEPTH, SGU_GROUPS, SGU_CHUNK, SGU_CHUNK), SGU_CHUNK ** -0.5),
        'sgu_b_s': gain((SGU_GROUPS, SGU_CHUNK)),
        'diff_lam_q1': nrm((DEPTH, DIFF_QK_DIM), 0.1),
        'diff_lam_k1': nrm((DEPTH, DIFF_QK_DIM), 0.1),
        'diff_lam_q2': nrm((DEPTH, DIFF_QK_DIM), 0.1),
        'diff_lam_k2': nrm((DEPTH, DIFF_QK_DIM), 0.1),
        'diff_sub_g': gain((DIFF_V_DIM,)),
        'w_mix_out': nrm((DEPTH, MIX_WIDTH, D_MODEL), MIX_WIDTH ** -0.5),
        'mem_pre_g': gain((D_MODEL,)),
        'mem_post_g': gain((D_MODEL,)),
        'mem_kv_g': gain((D_MODEL,)),
        'mem_w_q': nrm((DEPTH, D_MODEL, D_MODEL), D_MODEL ** -0.5),
        'mem_w_kv': nrm((DEPTH, D_MODEL, 2 * D_MODEL), D_MODEL ** -0.5),
        'mem_w_o': nrm((DEPTH, D_MODEL, D_MODEL), D_MODEL ** -0.5),
        'ffn_pre_g': gain((D_MODEL,)),
        'ffn_post_g': gain((D_MODEL,)),
        'ffn_w_up': nrm((DEPTH, D_MODEL, 2 * D_FF), D_MODEL ** -0.5),
        'ffn_conv_w': nrm((DEPTH, CONV_WIDTH, 2 * D_FF), CONV_WIDTH ** -0.5),
        'ffn_conv_b': nrm((DEPTH, 2 * D_FF), 0.01),
        'ffn_w_down': nrm((DEPTH, D_FF, D_MODEL), D_FF ** -0.5),
    }


def reference(x, mem, positions,
              mix_pre_g, mix_post_g, w_in, mla_cq_g, mla_ckv_g, mla_w_uq, mla_w_ukv,
              sgu_norm_g, sgu_w_s, sgu_b_s,
              diff_lam_q1, diff_lam_k1, diff_lam_q2, diff_lam_k2, diff_sub_g, w_mix_out,
              mem_pre_g, mem_post_g, mem_kv_g, mem_w_q, mem_w_kv, mem_w_o,
              ffn_pre_g, ffn_post_g, ffn_w_up, ffn_conv_w, ffn_conv_b, ffn_w_down):
    cos_a, sin_a = rope_tables(positions, MLA_ROPE)
    cos_d, sin_d = rope_tables(positions, 2 * DIFF_ROT)
    f32 = jnp.float32
    for l in range(DEPTH):
        lam_init = 0.8 - 0.6 * math.exp(-0.3 * l)
        h = rms_norm(x, mix_pre_g[l])
        proj = h @ w_in[l]
        cq, ckv, kr, z, dq, dk, dv = jnp.split(proj, IN_SPLIT, axis=-1)
        out_a = mla_group(cq, ckv, kr, mla_cq_g[l], mla_ckv_g[l], mla_w_uq[l], mla_w_ukv[l], cos_a, sin_a)
        out_b = sgu_group(z, sgu_norm_g[l], sgu_w_s[l], sgu_b_s[l])
        lam = (jnp.exp(jnp.sum(diff_lam_q1[l].astype(f32) * diff_lam_k1[l].astype(f32)))
               - jnp.exp(jnp.sum(diff_lam_q2[l].astype(f32) * diff_lam_k2[l].astype(f32)))
               + lam_init)
        out_c = diff_group(dq, dk, dv, lam, lam_init, diff_sub_g[l], cos_d, sin_d)
        mix = jnp.concatenate([out_a, out_b, out_c], axis=-1) @ w_mix_out[l]
        x = x + rms_norm(mix, mix_post_g[l])
        h = rms_norm(x, mem_pre_g[l])
        mem_n = rms_norm(mem, mem_kv_g[l])
        x = x + rms_norm(memory_cross_attention(h, mem_n, mem_w_q[l], mem_w_kv[l], mem_w_o[l]), mem_post_g[l])
        h = rms_norm(x, ffn_pre_g[l])
        x = x + rms_norm(conv_ffn(h, ffn_w_up[l], ffn_conv_w[l], ffn_conv_b[l], ffn_w_down[l]), ffn_post_g[l])
    return x
```

```python
import functools
import math

import jax
import jax.numpy as jnp
import numpy as np
from jax import lax
from jax.experimental import pallas as pl
from jax.experimental.pallas import tpu as pltpu

F32 = jnp.float32
BF16 = jnp.bfloat16

D_MODEL = 1024
DEPTH = 4
N_MEM = 256
ROPE_THETA = 500000.0
EPS = 1e-6

MLA_HEADS = 8
MLA_NOPE = 64
MLA_ROPE = 32
MLA_V = 64
MLA_Q_LORA = 384
MLA_KV_LORA = 256

SGU_GROUPS = 4
SGU_GROUP_DIM = 64
SGU_WIDTH = SGU_GROUPS * SGU_GROUP_DIM
SGU_CHUNK = 128

DIFF_HEADS = 4
DIFF_QK_DIM = 32
DIFF_V_DIM = 64
DIFF_ROT = 8
DIFF_GROUPS = 2 * DIFF_HEADS

X_HEADS = 4
X_HEAD_DIM = D_MODEL // X_HEADS

D_FF = 2816
CONV_WIDTH = 3

LANES = 128
HEAD_PAD = LANES
LOG2E = math.log2(math.e)
VMEM_LIMIT = 56 * 1024 * 1024

O_CQ = 0
O_CKV = O_CQ + MLA_Q_LORA
O_KR = O_CKV + MLA_KV_LORA
O_Z = O_KR + HEAD_PAD
O_DQ = O_Z + 2 * SGU_WIDTH
O_DK = O_DQ + 256
O_DV = O_DK + 256
IN_PAD = O_DV + 256

TM_IN = 256
TQ = 256
TM_MEM = 512
TM_FFN = 512
FFN_HALO = 16
FFN_TN = D_FF // 2


def _params(*sem):
    return pltpu.CompilerParams(dimension_semantics=sem, vmem_limit_bytes=VMEM_LIMIT)


def _rms(x, g):
    ms = jnp.mean(x * x, axis=-1, keepdims=True)
    return x * lax.rsqrt(ms + EPS) * g


def _gelu(x):
    c = math.sqrt(2.0 / math.pi)
    return 0.5 * x * (1.0 + jnp.tanh(c * (x + 0.044715 * (x * x * x))))


def _dot(a, b):
    return jnp.dot(a, b, preferred_element_type=F32)


def _dot_nt(a, b):
    return lax.dot_general(a, b, (((1,), (1,)), ((), ())), preferred_element_type=F32)


def _rope_kernel(pos_ref, fa_ref, ga_ref, fd_ref, gd_ref, ca_ref, sa_ref, cd_ref, sd_ref):
    pos = pos_ref[...].astype(F32)
    ang_a = pos * fa_ref[...]
    ca_ref[...] = jnp.cos(ang_a)
    sa_ref[...] = jnp.sin(ang_a) * ga_ref[...]
    ang_d = pos * fd_ref[...]
    cd_ref[...] = jnp.cos(ang_d)
    sd_ref[...] = jnp.sin(ang_d) * gd_ref[...]


def _rope_tables(pos, fa, ga, fd, gd):
    t = pos.shape[0]
    tm = 1024
    row = lambda w: pl.BlockSpec((tm, w), lambda i: (i, 0))
    vec = lambda w: pl.BlockSpec((1, w), lambda i: (0, 0))
    return pl.pallas_call(
        _rope_kernel,
        grid=(t // tm,),
        in_specs=[row(1), vec(LANES), vec(LANES), vec(256), vec(256)],
        out_specs=[row(LANES), row(LANES), row(256), row(256)],
        out_shape=[jax.ShapeDtypeStruct((t, LANES), F32), jax.ShapeDtypeStruct((t, LANES), F32),
                   jax.ShapeDtypeStruct((t, 256), F32), jax.ShapeDtypeStruct((t, 256), F32)],
        compiler_params=_params("parallel"),
        name="rope_tables",
    )(pos, fa, ga, fd, gd)


def _mem_kv_kernel(mem_ref, g_ref, w_ref, o_ref):
    hn = _rms(mem_ref[...], g_ref[...]).astype(BF16)
    o_ref[...] = _dot(hn, w_ref[...]).astype(BF16)


def _mem_kv(mem2, g, w):
    rows = mem2.shape[0]
    return pl.pallas_call(
        _mem_kv_kernel,
        grid=(rows // N_MEM,),
        in_specs=[pl.BlockSpec((N_MEM, D_MODEL), lambda i: (i, 0)),
                  pl.BlockSpec((1, D_MODEL), lambda i: (0, 0)),
                  pl.BlockSpec((D_MODEL, 2 * D_MODEL), lambda i: (0, 0))],
        out_specs=pl.BlockSpec((N_MEM, 2 * D_MODEL), lambda i: (i, 0)),
        out_shape=jax.ShapeDtypeStruct((rows, 2 * D_MODEL), BF16),
        compiler_params=_params("parallel"),
        name="mem_kv",
    )(mem2, g, w)


def _mix_in_kernel(x_ref, g_ref, win_ref, gcq_ref, gckv_ref, wuq_ref, wuk_ref, wuv_ref,
                   ca_ref, sa_ref, cd_ref, sd_ref, gsgu_ref, ws_ref, bs_ref,
                   qa_ref, ka_ref, va_ref, sgu_ref, qd_ref, kd_ref, vd_ref):
    tm = x_ref.shape[0]
    h = _rms(x_ref[...], g_ref[...]).astype(BF16)
    proj = _dot(h, win_ref[...])

    lane = lax.broadcasted_iota(jnp.int32, (tm, LANES), 1)
    low_half = lane < 64
    ones_tile = jnp.ones((tm, LANES), BF16)

    def place_v(v_all, head):
        vt = v_all[:, LANES * (head // 2):LANES * (head // 2 + 1)]
        keep = low_half if head % 2 == 0 else jnp.logical_not(low_half)
        return jnp.where(keep, vt, 0.0).astype(BF16)

    ca = ca_ref[...]
    sa = sa_ref[...]
    cqn = _rms(proj[:, O_CQ:O_CQ + MLA_Q_LORA], gcq_ref[...]).astype(BF16)
    q = _dot(cqn, wuq_ref[...])
    ckvn = _rms(proj[:, O_CKV:O_CKV + MLA_KV_LORA], gckv_ref[...]).astype(BF16)
    kn = _dot(ckvn, wuk_ref[...])
    vv = _dot(ckvn, wuv_ref[...])
    kr = proj[:, O_KR:O_KR + HEAD_PAD]
    kr = kr * ca + pltpu.roll(kr, 64, 1) * sa
    for hd in range(MLA_HEADS):
        qh = q[:, HEAD_PAD * hd:HEAD_PAD * (hd + 1)]
        qa_ref[hd] = (qh * ca + pltpu.roll(qh, 64, 1) * sa).astype(BF16)
        ka_ref[hd] = (kn[:, HEAD_PAD * hd:HEAD_PAD * (hd + 1)] + kr).astype(BF16)
        va_ref[hd, :, 0:LANES] = place_v(vv, hd)
        va_ref[hd, :, LANES:2 * LANES] = ones_tile

    zg = _gelu(proj[:, O_Z:O_Z + 2 * SGU_WIDTH])
    u = zg[:, :SGU_WIDTH]
    gsgu = gsgu_ref[...]
    inv_gd = 1.0 / SGU_GROUP_DIM
    vn_tiles = []
    for t in range(SGU_WIDTH // LANES):
        vt = zg[:, SGU_WIDTH + LANES * t:SGU_WIDTH + LANES * (t + 1)]
        s_lo = jnp.sum(jnp.where(low_half, vt, 0.0), axis=1, keepdims=True)
        s_hi = jnp.sum(jnp.where(low_half, 0.0, vt), axis=1, keepdims=True)
        xc = vt - jnp.where(low_half, s_lo, s_hi) * inv_gd
        sq = xc * xc
        v_lo = jnp.sum(jnp.where(low_half, sq, 0.0), axis=1, keepdims=True)
        v_hi = jnp.sum(jnp.where(low_half, 0.0, sq), axis=1, keepdims=True)
        var = jnp.where(low_half, v_lo, v_hi) * inv_gd
        vn_tiles.append((xc * lax.rsqrt(var + EPS) * gsgu[:, LANES * t:LANES * (t + 1)]).astype(BF16))
    bias = bs_ref[...]
    low_chunk = lax.broadcasted_iota(jnp.int32, (SGU_CHUNK, LANES), 1) < 64
    for n in range(tm // SGU_CHUNK):
        rows = slice(SGU_CHUNK * n, SGU_CHUNK * (n + 1))
        for t in range(SGU_WIDTH // LANES):
            vc = vn_tiles[t][rows, :]
            m_lo = _dot(ws_ref[2 * t], vc)
            m_hi = _dot(ws_ref[2 * t + 1], vc)
            mixed = jnp.where(low_chunk, m_lo, m_hi) + bias[:, LANES * t:LANES * (t + 1)]
            sgu_ref[rows, LANES * t:LANES * (t + 1)] = (u[rows, LANES * t:LANES * (t + 1)] * mixed).astype(BF16)

    cd = cd_ref[...]
    sd = sd_ref[...]
    dq = proj[:, O_DQ:O_DQ + 256]
    qd_ref[...] = (dq * cd + pltpu.roll(dq, LANES, 1) * sd).astype(BF16)
    dk = proj[:, O_DK:O_DK + 256]
    dk = dk * cd + pltpu.roll(dk, LANES, 1) * sd
    group = (lax.broadcasted_iota(jnp.int32, (tm, 256), 1) & 63) >> 3
    for g in range(DIFF_GROUPS):
        kd_ref[g] = jnp.where(group == g, dk, 0.0).astype(BF16)
    dv = proj[:, O_DV:O_DV + 256]
    for hd in range(DIFF_HEADS):
        vd_ref[hd, :, 0:LANES] = place_v(dv, hd)
        vd_ref[hd, :, LANES:2 * LANES] = ones_tile


def _mix_in(x, g, win, gcq, gckv, wuq, wuk, wuv, ca, sa, cd, sd, gsgu, ws, bs):
    t = x.shape[0]
    tm = TM_IN
    row = lambda w: pl.BlockSpec((tm, w), lambda i: (i, 0))
    full = lambda a: pl.BlockSpec(a.shape, lambda i: (0,) * a.ndim)
    heads = lambda n, w: pl.BlockSpec((n, tm, w), lambda i: (0, i, 0))
    return pl.pallas_call(
        _mix_in_kernel,
        grid=(t // tm,),
        in_specs=[row(D_MODEL), full(g), full(win), full(gcq), full(gckv), full(wuq), full(wuk), full(wuv),
                  row(LANES), row(LANES), row(256), row(256), full(gsgu), full(ws), full(bs)],
        out_specs=[heads(MLA_HEADS, HEAD_PAD), heads(MLA_HEADS, HEAD_PAD), heads(MLA_HEADS, 2 * LANES),
                   row(SGU_WIDTH), row(256), heads(DIFF_GROUPS, 256), heads(DIFF_HEADS, 2 * LANES)],
        out_shape=[jax.ShapeDtypeStruct((MLA_HEADS, t, HEAD_PAD), BF16),
                   jax.ShapeDtypeStruct((MLA_HEADS, t, HEAD_PAD), BF16),
                   jax.ShapeDtypeStruct((MLA_HEADS, t, 2 * LANES), BF16),
                   jax.ShapeDtypeStruct((t, SGU_WIDTH), BF16),
                   jax.ShapeDtypeStruct((t, 256), BF16),
                   jax.ShapeDtypeStruct((DIFF_GROUPS, t, 256), BF16),
                   jax.ShapeDtypeStruct((DIFF_HEADS, t, 2 * LANES), BF16)],
        compiler_params=_params("parallel"),
        name="mix_in",
    )(x, g, win, gcq, gckv, wuq, wuk, wuv, ca, sa, cd, sd, gsgu, ws, bs)


def _softmax_pv(q, k, v_ext, scale):
    s = _dot_nt(q, k)
    m = jnp.max(s, axis=1, keepdims=True)
    e = jnp.exp2((s - m) * (scale * LOG2E)).astype(BF16)
    r = _dot(e, v_ext)
    return r[:, :LANES] / r[:, LANES:]


def _mla_attn_kernel(q_ref, k_ref, v_ref, o_ref):
    scale = (MLA_NOPE + MLA_ROPE) ** -0.5
    o = _softmax_pv(q_ref[0], k_ref[0], v_ref[0], scale)
    o = o + _softmax_pv(q_ref[1], k_ref[1], v_ref[1], scale)
    o_ref[...] = o.astype(BF16)


def _mla_attn(qa, ka, va, batch, seq):
    t = qa.shape[1]
    nq = seq // TQ
    return pl.pallas_call(
        _mla_attn_kernel,
        grid=(batch, MLA_HEADS // 2, nq),
        in_specs=[pl.BlockSpec((2, TQ, HEAD_PAD), lambda b, j, i: (j, b * nq + i, 0)),
                  pl.BlockSpec((2, seq, HEAD_PAD), lambda b, j, i: (j, b, 0)),
                  pl.BlockSpec((2, seq, 2 * LANES), lambda b, j, i: (j, b, 0))],
        out_specs=pl.BlockSpec((TQ, LANES), lambda b, j, i: (b * nq + i, j)),
        out_shape=jax.ShapeDtypeStruct((t, MLA_HEADS * MLA_V), BF16),
        compiler_params=_params("parallel", "parallel", "arbitrary"),
        name="mla_attn",
    )(qa, ka, va)


def _diff_attn_kernel(lam_ref, linit_ref, gsub_ref, q_ref, k_ref, v_ref, o_ref):
    scale = DIFF_QK_DIM ** -0.5
    lam_init = linit_ref[...]
    lam = (jnp.exp(jnp.sum(lam_ref[0:1, :] * lam_ref[1:2, :], axis=1, keepdims=True))
           - jnp.exp(jnp.sum(lam_ref[2:3, :] * lam_ref[3:4, :], axis=1, keepdims=True)) + lam_init)
    gsub = gsub_ref[...] * (1.0 - lam_init)
    q = q_ref[...]
    for pair in range(DIFF_HEADS // 2):
        acc = None
        for hd in (2 * pair, 2 * pair + 1):
            o1 = _softmax_pv(q, k_ref[2 * hd], v_ref[hd], scale)
            o2 = _softmax_pv(q, k_ref[2 * hd + 1], v_ref[hd], scale)
            o = o1 - lam * o2
            ms = jnp.sum(o * o, axis=1, keepdims=True) * (1.0 / DIFF_V_DIM)
            on = o * lax.rsqrt(ms + EPS) * gsub
            acc = on if acc is None else acc + on
        o_ref[:, LANES * pair:LANES * (pair + 1)] = acc.astype(BF16)


def _diff_attn(lamv, linit, gsub, qd, kd, vd, batch, seq):
    t = qd.shape[0]
    nq = seq // TQ
    full = lambda a: pl.BlockSpec(a.shape, lambda b, i: (0,) * a.ndim)
    return pl.pallas_call(
        _diff_attn_kernel,
        grid=(batch, nq),
        in_specs=[full(lamv), full(linit), full(gsub),
                  pl.BlockSpec((TQ, 256), lambda b, i: (b * nq + i, 0)),
                  pl.BlockSpec((DIFF_GROUPS, seq, 256), lambda b, i: (0, b, 0)),
                  pl.BlockSpec((DIFF_HEADS, seq, 2 * LANES), lambda b, i: (0, b, 0))],
        out_specs=pl.BlockSpec((TQ, 256), lambda b, i: (b * nq + i, 0)),
        out_shape=jax.ShapeDtypeStruct((t, DIFF_HEADS * DIFF_V_DIM), BF16),
        compiler_params=_params("parallel", "arbitrary"),
        name="diff_attn",
    )(lamv, linit, gsub, qd, kd, vd)


def _mix_mem_kernel(x_ref, oa_ref, ob_ref, oc_ref, wmix_ref, gmix_ref, gpre_ref, wq_ref,
                    k_ref, v_ref, wo_ref, gpost_ref, o_ref):
    cat = jnp.concatenate([oa_ref[...], ob_ref[...], oc_ref[...]], axis=1)
    x1 = x_ref[...] + _rms(_dot(cat, wmix_ref[...]), gmix_ref[...])
    hq = _rms(x1, gpre_ref[...]).astype(BF16)
    q = _dot(hq, wq_ref[...]).astype(BF16)
    c = (X_HEAD_DIM ** -0.5) * LOG2E
    heads = []
    for hd in range(X_HEADS):
        cols = slice(X_HEAD_DIM * hd, X_HEAD_DIM * (hd + 1))
        s = _dot_nt(q[:, cols], k_ref[:, cols])
        m = jnp.max(s, axis=1, keepdims=True)
        e = jnp.exp2((s - m) * c)
        den = jnp.sum(e, axis=1, keepdims=True)
        heads.append((_dot(e.astype(BF16), v_ref[:, cols]) / den).astype(BF16))
    o = _dot(jnp.concatenate(heads, axis=1), wo_ref[...])
    o_ref[...] = x1 + _rms(o, gpost_ref[...])


def _mix_mem(x, oa, ob, oc, wmix, gmix, gpre, wq, kv, wo, gpost, seq):
    t = x.shape[0]
    tm = TM_MEM
    per_b = seq // tm
    row = lambda w: pl.BlockSpec((tm, w), lambda i: (i, 0))
    full = lambda a: pl.BlockSpec(a.shape, lambda i: (0,) * a.ndim)
    return pl.pallas_call(
        _mix_mem_kernel,
        grid=(t // tm,),
        in_specs=[row(D_MODEL), row(MLA_HEADS * MLA_V), row(SGU_WIDTH), row(DIFF_HEADS * DIFF_V_DIM),
                  full(wmix), full(gmix), full(gpre), full(wq),
                  pl.BlockSpec((N_MEM, D_MODEL), lambda i: (i // per_b, 0)),
                  pl.BlockSpec((N_MEM, D_MODEL), lambda i: (i // per_b, 1)),
                  full(wo), full(gpost)],
        out_specs=row(D_MODEL),
        out_shape=jax.ShapeDtypeStruct((t, D_MODEL), F32),
        compiler_params=_params("parallel"),
        name="mix_mem",
    )(x, oa, ob, oc, wmix, gmix, gpre, wq, kv, kv, wo, gpost)


def _ffn_up_kernel(x_ref, xp_ref, xn_ref, g_ref, wg_ref, wu_ref, cwg_ref, cwu_ref, cbg_ref, cbu_ref,
                   o_ref, h_ref, *, seq):
    tm = x_ref.shape[0]
    i = pl.program_id(0)

    @pl.when(pl.program_id(1) == 0)
    def _():
        g = g_ref[...]
        at_start = (i * tm) % seq == 0
        at_end = ((i + 1) * tm) % seq == 0
        h_ref[0:FFN_HALO, :] = jnp.where(at_start, 0.0, _rms(xp_ref[...], g)).astype(BF16)
        h_ref[FFN_HALO:FFN_HALO + tm, :] = _rms(x_ref[...], g).astype(BF16)
        h_ref[FFN_HALO + tm:, :] = jnp.where(at_end, 0.0, _rms(xn_ref[...], g)).astype(BF16)

    hb = h_ref[...]
    rows = tm + 2 * FFN_HALO

    def conv(w_ref, cw_ref, cb_ref):
        a = _dot(hb, w_ref[...])
        cw = cw_ref[...]
        c = cw[0:1] * pltpu.roll(a, 1, 0) + cw[1:2] * a + cw[2:3] * pltpu.roll(a, rows - 1, 0)
        return c[FFN_HALO:FFN_HALO + tm] + cb_ref[...]

    gate = conv(wg_ref, cwg_ref, cbg_ref)
    up = conv(wu_ref, cwu_ref, cbu_ref)
    o_ref[...] = (_gelu(gate) * up).astype(BF16)


def _ffn_up(x, g, wup, cw, cb, seq):
    t = x.shape[0]
    tm = TM_FFN
    tn = FFN_TN
    nj = D_FF // tn
    hb = tm // FFN_HALO
    last = t // FFN_HALO - 1
    return pl.pallas_call(
        functools.partial(_ffn_up_kernel, seq=seq),
        grid=(t // tm, nj),
        in_specs=[pl.BlockSpec((tm, D_MODEL), lambda i, j: (i, 0)),
                  pl.BlockSpec((FFN_HALO, D_MODEL), lambda i, j: (jnp.maximum(i * hb - 1, 0), 0)),
                  pl.BlockSpec((FFN_HALO, D_MODEL), lambda i, j: (jnp.minimum((i + 1) * hb, last), 0)),
                  pl.BlockSpec((1, D_MODEL), lambda i, j: (0, 0)),
                  pl.BlockSpec((D_MODEL, tn), lambda i, j: (0, j)),
                  pl.BlockSpec((D_MODEL, tn), lambda i, j: (0, j + nj)),
                  pl.BlockSpec((CONV_WIDTH, tn), lambda i, j: (0, j)),
                  pl.BlockSpec((CONV_WIDTH, tn), lambda i, j: (0, j + nj)),
                  pl.BlockSpec((1, tn), lambda i, j: (0, j)),
                  pl.BlockSpec((1, tn), lambda i, j: (0, j + nj))],
        out_specs=pl.BlockSpec((tm, tn), lambda i, j: (i, j)),
        out_shape=jax.ShapeDtypeStruct((t, D_FF), BF16),
        scratch_shapes=[pltpu.VMEM((tm + 2 * FFN_HALO, D_MODEL), BF16)],
        compiler_params=_params("parallel", "arbitrary"),
        name="ffn_up",
    )(x, x, x, g, wup, wup, cw, cw, cb, cb)


def _ffn_down_kernel(x_ref, a_ref, w_ref, g_ref, o_ref):
    o_ref[...] = x_ref[...] + _rms(_dot(a_ref[...], w_ref[...]), g_ref[...])


def _ffn_down(x, a, w, g):
    t = x.shape[0]
    tm = TM_FFN
    return pl.pallas_call(
        _ffn_down_kernel,
        grid=(t // tm,),
        in_specs=[pl.BlockSpec((tm, D_MODEL), lambda i: (i, 0)),
                  pl.BlockSpec((tm, D_FF), lambda i: (i, 0)),
                  pl.BlockSpec((D_FF, D_MODEL), lambda i: (0, 0)),
                  pl.BlockSpec((1, D_MODEL), lambda i: (0, 0))],
        out_specs=pl.BlockSpec((tm, D_MODEL), lambda i: (i, 0)),
        out_shape=jax.ShapeDtypeStruct((t, D_MODEL), F32),
        compiler_params=_params("parallel"),
        name="ffn_down",
    )(x, a, w, g)


def _take_cols(w, idx):
    wz = jnp.concatenate([w, jnp.zeros(w.shape[:-1] + (1,), w.dtype)], axis=-1)
    return jnp.take(wz, jnp.asarray(np.where(idx < 0, w.shape[-1], idx)), axis=-1)


def _mla_head_lanes():
    src = np.full((HEAD_PAD,), -1, np.int64)
    src[0:16] = MLA_NOPE + np.arange(16)
    src[16:64] = np.arange(48)
    src[64:80] = MLA_NOPE + 16 + np.arange(16)
    src[80:96] = 48 + np.arange(16)
    return src


def _diff_lanes():
    src = np.zeros((256,), np.int64)
    for g in range(DIFF_GROUPS):
        base = g * DIFF_QK_DIM
        for i in range(DIFF_ROT):
            src[g * 8 + i] = base + i
            src[128 + g * 8 + i] = base + DIFF_ROT + i
            src[64 + g * 8 + i] = base + 2 * DIFF_ROT + i
            src[192 + g * 8 + i] = base + 3 * DIFF_ROT + i
    return src


def _layouts():
    head = _mla_head_lanes()
    p_cq, p_ckv, p_kr, p_sgu = MLA_Q_LORA, MLA_KV_LORA, MLA_ROPE, 2 * SGU_WIDTH
    o_kr = p_cq + p_ckv
    o_z = o_kr + p_kr
    o_dq = o_z + p_sgu
    o_dk = o_dq + 256
    o_dv = o_dk + 256
    kr_src = np.where(head >= MLA_NOPE, o_kr + head - MLA_NOPE, -1)
    dl = _diff_lanes()
    win_idx = np.concatenate([np.arange(0, o_kr), kr_src, np.arange(o_z, o_dq), o_dq + dl, o_dk + dl,
                              np.arange(o_dv, o_dv + 256)])
    assert win_idx.shape[0] == IN_PAD
    per_q = MLA_NOPE + MLA_ROPE
    wuq_idx = np.concatenate([np.where(head >= 0, h * per_q + head, -1) for h in range(MLA_HEADS)])
    per_kv = MLA_NOPE + MLA_V
    k_src = np.where((head >= 0) & (head < MLA_NOPE), head, -1)
    wuk_idx = np.concatenate([np.where(k_src >= 0, h * per_kv + k_src, -1) for h in range(MLA_HEADS)])
    wuv_idx = np.concatenate([h * per_kv + MLA_NOPE + np.arange(MLA_V) for h in range(MLA_HEADS)])
    inv_a = ROPE_THETA ** (-jnp.arange(0, MLA_ROPE, 2, dtype=F32) / MLA_ROPE)
    inv_d = ROPE_THETA ** (-jnp.arange(0, 2 * DIFF_ROT, 2, dtype=F32) / (2 * DIFF_ROT))
    fa = jnp.zeros((LANES,), F32).at[0:16].set(inv_a).at[64:80].set(inv_a)
    ga = jnp.zeros((LANES,), F32).at[0:16].set(-1.0).at[64:80].set(1.0)
    fd = jnp.zeros((256,), F32).at[0:64].set(jnp.tile(inv_d, 8)).at[128:192].set(jnp.tile(inv_d, 8))
    gd = jnp.zeros((256,), F32).at[0:64].set(-1.0).at[128:192].set(1.0)
    return win_idx, wuq_idx, wuk_idx, wuv_idx, fa[None], ga[None], fd[None], gd[None]


def kernel(x, mem, positions, mix_pre_g, mix_post_g, w_in, mla_cq_g, mla_ckv_g, mla_w_uq, mla_w_ukv,
           sgu_norm_g, sgu_w_s, sgu_b_s, diff_lam_q1, diff_lam_k1, diff_lam_q2, diff_lam_k2, diff_sub_g,
           w_mix_out, mem_pre_g, mem_post_g, mem_kv_g, mem_w_q, mem_w_kv, mem_w_o,
           ffn_pre_g, ffn_post_g, ffn_w_up, ffn_conv_w, ffn_conv_b, ffn_w_down):
    batch, seq, d = x.shape
    t = batch * seq
    assert d == D_MODEL and seq % TM_FFN == 0 and seq % TQ == 0 and seq % SGU_CHUNK == 0
    win_idx, wuq_idx, wuk_idx, wuv_idx, fa, ga, fd, gd = _layouts()

    ca, sa, cd, sd = _rope_tables(positions.reshape(t, 1), fa, ga, fd, gd)

    vec = lambda a: a.astype(F32)[:, None, :]
    lam_init = np.array([0.8 - 0.6 * math.exp(-0.3 * l) for l in range(DEPTH)], np.float32)
    layers = dict(
        g_mix_pre=vec(mix_pre_g), g_mix_post=vec(mix_post_g),
        w_in=_take_cols(w_in, win_idx).astype(BF16),
        g_cq=vec(mla_cq_g), g_ckv=vec(mla_ckv_g),
        w_uq=_take_cols(mla_w_uq, wuq_idx).astype(BF16),
        w_uk=_take_cols(mla_w_ukv, wuk_idx).astype(BF16),
        w_uv=_take_cols(mla_w_ukv, wuv_idx).astype(BF16),
        g_sgu=vec(sgu_norm_g.reshape(DEPTH, SGU_WIDTH)),
        w_s=sgu_w_s.astype(BF16),
        b_s=jnp.repeat(jnp.swapaxes(sgu_b_s.astype(F32), 1, 2), SGU_GROUP_DIM, axis=2),
        lam=jnp.stack([diff_lam_q1, diff_lam_k1, diff_lam_q2, diff_lam_k2], axis=1).astype(F32),
        lam_init=jnp.asarray(lam_init).reshape(DEPTH, 1, 1),
        g_sub=vec(jnp.tile(diff_sub_g, (1, 2))),
        w_mix=w_mix_out.astype(BF16),
        g_mem_pre=vec(mem_pre_g), g_mem_post=vec(mem_post_g), g_mem_kv=vec(mem_kv_g),
        w_mq=mem_w_q.astype(BF16), w_mkv=mem_w_kv.astype(BF16), w_mo=mem_w_o.astype(BF16),
        g_ffn_pre=vec(ffn_pre_g), g_ffn_post=vec(ffn_post_g),
        w_up=ffn_w_up.astype(BF16), conv_w=ffn_conv_w.astype(F32), conv_b=vec(ffn_conv_b),
        w_down=ffn_w_down.astype(BF16),
    )
    mem2 = mem.reshape(batch * N_MEM, D_MODEL)

    def layer(xc, p):
        qa, ka, va, sgu, qd, kd, vd = _mix_in(
            xc, p["g_mix_pre"], p["w_in"], p["g_cq"], p["g_ckv"], p["w_uq"], p["w_uk"], p["w_uv"],
            ca, sa, cd, sd, p["g_sgu"], p["w_s"], p["b_s"])
        out_a = _mla_attn(qa, ka, va, batch, seq)
        out_c = _diff_attn(p["lam"], p["lam_init"], p["g_sub"], qd, kd, vd, batch, seq)
        kv = _mem_kv(mem2, p["g_mem_kv"], p["w_mkv"])
        x2 = _mix_mem(xc, out_a, sgu, out_c, p["w_mix"], p["g_mix_post"], p["g_mem_pre"], p["w_mq"],
                      kv, p["w_mo"], p["g_mem_post"], seq)
        mid = _ffn_up(x2, p["g_ffn_pre"], p["w_up"], p["conv_w"], p["conv_b"], seq)
        x3 = _ffn_down(x2, mid, p["w_down"], p["g_ffn_post"])
        return x3, None

    out, _ = lax.scan(layer, x.reshape(t, D_MODEL), layers)
    return out.reshape(batch, seq, D_MODEL)
```

```python
import functools
import math

import jax
import jax.numpy as jnp
import numpy as np
from jax import lax
from jax.experimental import pallas as pl
from jax.experimental.pallas import tpu as pltpu

F32 = jnp.float32
BF16 = jnp.bfloat16

D_MODEL = 1024
DEPTH = 4
N_MEM = 256
ROPE_THETA = 500000.0
EPS = 1e-6

MLA_HEADS = 8
MLA_NOPE = 64
MLA_ROPE = 32
MLA_V = 64
MLA_Q_LORA = 384
MLA_KV_LORA = 256

SGU_GROUPS = 4
SGU_GROUP_DIM = 64
SGU_WIDTH = SGU_GROUPS * SGU_GROUP_DIM
SGU_CHUNK = 128

DIFF_HEADS = 4
DIFF_QK_DIM = 32
DIFF_V_DIM = 64
DIFF_ROT = 8
DIFF_GROUPS = 2 * DIFF_HEADS

X_HEADS = 4
X_HEAD_DIM = D_MODEL // X_HEADS

D_FF = 2816
CONV_WIDTH = 3

LANES = 128
HEAD_PAD = LANES
LOG2E = math.log2(math.e)
VMEM_LIMIT = 56 * 1024 * 1024

O_CQ = 0
O_CKV = O_CQ + MLA_Q_LORA
O_KR = O_CKV + MLA_KV_LORA
O_Z = O_KR + HEAD_PAD
O_DQ = O_Z + 2 * SGU_WIDTH
O_DK = O_DQ + 256
O_DV = O_DK + 256
IN_PAD = O_DV + 256

TM_IN = 256
TQ = 512
TM_MEM = 512
TM_FFN = 512
FFN_HALO = 16
FFN_TN = D_FF // 2


def _params(*sem):
    return pltpu.CompilerParams(dimension_semantics=sem, vmem_limit_bytes=VMEM_LIMIT)


def _rms(x, g):
    ms = jnp.mean(x * x, axis=-1, keepdims=True)
    return x * lax.rsqrt(ms + EPS) * g


def _gelu(x):
    c = math.sqrt(2.0 / math.pi)
    return 0.5 * x * (1.0 + jnp.tanh(c * (x + 0.044715 * (x * x * x))))


def _dot(a, b):
    return jnp.dot(a, b, preferred_element_type=F32)


def _dot_nt(a, b):
    return lax.dot_general(a, b, (((1,), (1,)), ((), ())), preferred_element_type=F32)


def _rope_kernel(pos_ref, fa_ref, ga_ref, fd_ref, gd_ref, ca_ref, sa_ref, cd_ref, sd_ref):
    pos = pos_ref[...].astype(F32)
    ang_a = pos * fa_ref[...]
    ca_ref[...] = jnp.cos(ang_a)
    sa_ref[...] = jnp.sin(ang_a) * ga_ref[...]
    ang_d = pos * fd_ref[...]
    cd_ref[...] = jnp.cos(ang_d)
    sd_ref[...] = jnp.sin(ang_d) * gd_ref[...]


def _rope_tables(pos, fa, ga, fd, gd):
    t = pos.shape[0]
    tm = 1024
    row = lambda w: pl.BlockSpec((tm, w), lambda i: (i, 0))
    vec = lambda w: pl.BlockSpec((1, w), lambda i: (0, 0))
    return pl.pallas_call(
        _rope_kernel,
        grid=(t // tm,),
        in_specs=[row(1), vec(LANES), vec(LANES), vec(256), vec(256)],
        out_specs=[row(LANES), row(LANES), row(256), row(256)],
        out_shape=[jax.ShapeDtypeStruct((t, LANES), F32), jax.ShapeDtypeStruct((t, LANES), F32),
                   jax.ShapeDtypeStruct((t, 256), F32), jax.ShapeDtypeStruct((t, 256), F32)],
        compiler_params=_params("parallel"),
        name="rope_tables",
    )(pos, fa, ga, fd, gd)


def _mem_kv_kernel(mem_ref, g_ref, w_ref, o_ref):
    hn = _rms(mem_ref[...], g_ref[...]).astype(BF16)
    o_ref[...] = _dot(hn, w_ref[...]).astype(BF16)


def _mem_kv(mem2, g, w):
    rows = mem2.shape[0]
    return pl.pallas_call(
        _mem_kv_kernel,
        grid=(rows // N_MEM,),
        in_specs=[pl.BlockSpec((N_MEM, D_MODEL), lambda i: (i, 0)),
                  pl.BlockSpec((1, D_MODEL), lambda i: (0, 0)),
                  pl.BlockSpec((D_MODEL, 2 * D_MODEL), lambda i: (0, 0))],
        out_specs=pl.BlockSpec((N_MEM, 2 * D_MODEL), lambda i: (i, 0)),
        out_shape=jax.ShapeDtypeStruct((rows, 2 * D_MODEL), BF16),
        compiler_params=_params("parallel"),
        name="mem_kv",
    )(mem2, g, w)


def _mix_in_kernel(x_ref, g_ref, win_ref, gcq_ref, gckv_ref, wuq_ref, wuk_ref, wuv_ref,
                   ca_ref, sa_ref, cd_ref, sd_ref, gsgu_ref, ws_ref, bs_ref,
                   qa_ref, ka_ref, va_ref, sgu_ref, qd_ref, kd_ref, vd_ref):
    tm = x_ref.shape[0]
    h = _rms(x_ref[...], g_ref[...]).astype(BF16)
    proj = _dot(h, win_ref[...])

    lane = lax.broadcasted_iota(jnp.int32, (tm, LANES), 1)
    low_half = lane < 64
    ones_tile = jnp.ones((tm, LANES), BF16)

    def place_v(v_all, head):
        vt = v_all[:, LANES * (head // 2):LANES * (head // 2 + 1)]
        keep = low_half if head % 2 == 0 else jnp.logical_not(low_half)
        return jnp.where(keep, vt, 0.0).astype(BF16)

    ca = ca_ref[...]
    sa = sa_ref[...]
    cqn = _rms(proj[:, O_CQ:O_CQ + MLA_Q_LORA], gcq_ref[...]).astype(BF16)
    q = _dot(cqn, wuq_ref[...])
    ckvn = _rms(proj[:, O_CKV:O_CKV + MLA_KV_LORA], gckv_ref[...]).astype(BF16)
    kn = _dot(ckvn, wuk_ref[...])
    vv = _dot(ckvn, wuv_ref[...])
    kr = proj[:, O_KR:O_KR + HEAD_PAD]
    kr = kr * ca + pltpu.roll(kr, 64, 1) * sa
    for hd in range(MLA_HEADS):
        qh = q[:, HEAD_PAD * hd:HEAD_PAD * (hd + 1)]
        qa_ref[hd] = (qh * ca + pltpu.roll(qh, 64, 1) * sa).astype(BF16)
        ka_ref[hd] = (kn[:, HEAD_PAD * hd:HEAD_PAD * (hd + 1)] + kr).astype(BF16)
        va_ref[hd, :, 0:LANES] = place_v(vv, hd)
        va_ref[hd, :, LANES:2 * LANES] = ones_tile

    zg = _gelu(proj[:, O_Z:O_Z + 2 * SGU_WIDTH])
    u = zg[:, :SGU_WIDTH]
    gsgu = gsgu_ref[...]
    inv_gd = 1.0 / SGU_GROUP_DIM
    vn_tiles = []
    for t in range(SGU_WIDTH // LANES):
        vt = zg[:, SGU_WIDTH + LANES * t:SGU_WIDTH + LANES * (t + 1)]
        s_lo = jnp.sum(jnp.where(low_half, vt, 0.0), axis=1, keepdims=True)
        s_hi = jnp.sum(jnp.where(low_half, 0.0, vt), axis=1, keepdims=True)
        xc = vt - jnp.where(low_half, s_lo, s_hi) * inv_gd
        sq = xc * xc
        v_lo = jnp.sum(jnp.where(low_half, sq, 0.0), axis=1, keepdims=True)
        v_hi = jnp.sum(jnp.where(low_half, 0.0, sq), axis=1, keepdims=True)
        var = jnp.where(low_half, v_lo, v_hi) * inv_gd
        vn_tiles.append((xc * lax.rsqrt(var + EPS) * gsgu[:, LANES * t:LANES * (t + 1)]).astype(BF16))
    bias = bs_ref[...]
    low_chunk = lax.broadcasted_iota(jnp.int32, (SGU_CHUNK, LANES), 1) < 64
    for n in range(tm // SGU_CHUNK):
        rows = slice(SGU_CHUNK * n, SGU_CHUNK * (n + 1))
        for t in range(SGU_WIDTH // LANES):
            vc = vn_tiles[t][rows, :]
            m_lo = _dot(ws_ref[2 * t], vc)
            m_hi = _dot(ws_ref[2 * t + 1], vc)
            mixed = jnp.where(low_chunk, m_lo, m_hi) + bias[:, LANES * t:LANES * (t + 1)]
            sgu_ref[rows, LANES * t:LANES * (t + 1)] = (u[rows, LANES * t:LANES * (t + 1)] * mixed).astype(BF16)

    cd = cd_ref[...]
    sd = sd_ref[...]
    dq = proj[:, O_DQ:O_DQ + 256]
    qd_ref[...] = (dq * cd + pltpu.roll(dq, LANES, 1) * sd).astype(BF16)
    dk = proj[:, O_DK:O_DK + 256]
    dk = dk * cd + pltpu.roll(dk, LANES, 1) * sd
    group = (lax.broadcasted_iota(jnp.int32, (tm, 256), 1) & 63) >> 3
    for g in range(DIFF_GROUPS):
        kd_ref[g] = jnp.where(group == g, dk, 0.0).astype(BF16)
    dv = proj[:, O_DV:O_DV + 256]
    for hd in range(DIFF_HEADS):
        vd_ref[hd, :, 0:LANES] = place_v(dv, hd)
        vd_ref[hd, :, LANES:2 * LANES] = ones_tile


def _mix_in(x, g, win, gcq, gckv, wuq, wuk, wuv, ca, sa, cd, sd, gsgu, ws, bs):
    t = x.shape[0]
    tm = TM_IN
    row = lambda w: pl.BlockSpec((tm, w), lambda i: (i, 0))
    full = lambda a: pl.BlockSpec(a.shape, lambda i: (0,) * a.ndim)
    heads = lambda n, w: pl.BlockSpec((n, tm, w), lambda i: (0, i, 0))
    return pl.pallas_call(
        _mix_in_kernel,
        grid=(t // tm,),
        in_specs=[row(D_MODEL), full(g), full(win), full(gcq), full(gckv), full(wuq), full(wuk), full(wuv),
                  row(LANES), row(LANES), row(256), row(256), full(gsgu), full(ws), full(bs)],
        out_specs=[heads(MLA_HEADS, HEAD_PAD), heads(MLA_HEADS, HEAD_PAD), heads(MLA_HEADS, 2 * LANES),
                   row(SGU_WIDTH), row(256), heads(DIFF_GROUPS, 256), heads(DIFF_HEADS, 2 * LANES)],
        out_shape=[jax.ShapeDtypeStruct((MLA_HEADS, t, HEAD_PAD), BF16),
                   jax.ShapeDtypeStruct((MLA_HEADS, t, HEAD_PAD), BF16),
                   jax.ShapeDtypeStruct((MLA_HEADS, t, 2 * LANES), BF16),
                   jax.ShapeDtypeStruct((t, SGU_WIDTH), BF16),
                   jax.ShapeDtypeStruct((t, 256), BF16),
                   jax.ShapeDtypeStruct((DIFF_GROUPS, t, 256), BF16),
                   jax.ShapeDtypeStruct((DIFF_HEADS, t, 2 * LANES), BF16)],
        compiler_params=_params("parallel"),
        name="mix_in",
    )(x, g, win, gcq, gckv, wuq, wuk, wuv, ca, sa, cd, sd, gsgu, ws, bs)


def _softmax_pv(q, k, v_ext, scale):
    s = _dot_nt(q, k)
    m = jnp.max(s, axis=1, keepdims=True)
    e = jnp.exp2((s - m) * (scale * LOG2E)).astype(BF16)
    r = _dot(e, v_ext)
    return r[:, :LANES] / r[:, LANES:]


def _mla_attn_kernel(q_ref, k_ref, v_ref, o_ref):
    scale = (MLA_NOPE + MLA_ROPE) ** -0.5
    for pair in range(MLA_HEADS // 2):
        o = _softmax_pv(q_ref[2 * pair], k_ref[2 * pair], v_ref[2 * pair], scale)
        o = o + _softmax_pv(q_ref[2 * pair + 1], k_ref[2 * pair + 1], v_ref[2 * pair + 1], scale)
        o_ref[:, LANES * pair:LANES * (pair + 1)] = o.astype(BF16)


def _mla_attn(qa, ka, va, batch, seq):
    t = qa.shape[1]
    nq = seq // TQ
    return pl.pallas_call(
        _mla_attn_kernel,
        grid=(batch, nq),
        in_specs=[pl.BlockSpec((MLA_HEADS, TQ, HEAD_PAD), lambda b, i: (0, b * nq + i, 0)),
                  pl.BlockSpec((MLA_HEADS, seq, HEAD_PAD), lambda b, i: (0, b, 0)),
                  pl.BlockSpec((MLA_HEADS, seq, 2 * LANES), lambda b, i: (0, b, 0))],
        out_specs=pl.BlockSpec((TQ, MLA_HEADS * MLA_V), lambda b, i: (b * nq + i, 0)),
        out_shape=jax.ShapeDtypeStruct((t, MLA_HEADS * MLA_V), BF16),
        compiler_params=_params("parallel", "arbitrary"),
        name="mla_attn",
    )(qa, ka, va)


def _diff_attn_kernel(lam_ref, linit_ref, gsub_ref, q_ref, k_ref, v_ref, o_ref):
    scale = DIFF_QK_DIM ** -0.5
    lam_init = linit_ref[...]
    lam = (jnp.exp(jnp.sum(lam_ref[0:1, :] * lam_ref[1:2, :], axis=1, keepdims=True))
           - jnp.exp(jnp.sum(lam_ref[2:3, :] * lam_ref[3:4, :], axis=1, keepdims=True)) + lam_init)
    gsub = gsub_ref[...] * (1.0 - lam_init)
    q = q_ref[...]
    for pair in range(DIFF_HEADS // 2):
        acc = None
        for hd in (2 * pair, 2 * pair + 1):
            o1 = _softmax_pv(q, k_ref[2 * hd], v_ref[hd], scale)
            o2 = _softmax_pv(q, k_ref[2 * hd + 1], v_ref[hd], scale)
            o = o1 - lam * o2
            ms = jnp.sum(o * o, axis=1, keepdims=True) * (1.0 / DIFF_V_DIM)
            on = o * lax.rsqrt(ms + EPS) * gsub
            acc = on if acc is None else acc + on
        o_ref[:, LANES * pair:LANES * (pair + 1)] = acc.astype(BF16)


def _diff_attn(lamv, linit, gsub, qd, kd, vd, batch, seq):
    t = qd.shape[0]
    nq = seq // TQ
    full = lambda a: pl.BlockSpec(a.shape, lambda b, i: (0,) * a.ndim)
    return pl.pallas_call(
        _diff_attn_kernel,
        grid=(batch, nq),
        in_specs=[full(lamv), full(linit), full(gsub),
                  pl.BlockSpec((TQ, 256), lambda b, i: (b * nq + i, 0)),
                  pl.BlockSpec((DIFF_GROUPS, seq, 256), lambda b, i: (0, b, 0)),
                  pl.BlockSpec((DIFF_HEADS, seq, 2 * LANES), lambda b, i: (0, b, 0))],
        out_specs=pl.BlockSpec((TQ, 256), lambda b, i: (b * nq + i, 0)),
        out_shape=jax.ShapeDtypeStruct((t, DIFF_HEADS * DIFF_V_DIM), BF16),
        compiler_params=_params("parallel", "arbitrary"),
        name="diff_attn",
    )(lamv, linit, gsub, qd, kd, vd)


def _mix_mem_kernel(x_ref, oa_ref, ob_ref, oc_ref, wmix_ref, gmix_ref, gpre_ref, wq_ref,
                    k_ref, v_ref, wo_ref, gpost_ref, o_ref):
    cat = jnp.concatenate([oa_ref[...], ob_ref[...], oc_ref[...]], axis=1)
    x1 = x_ref[...] + _rms(_dot(cat, wmix_ref[...]), gmix_ref[...])
    hq = _rms(x1, gpre_ref[...]).astype(BF16)
    q = _dot(hq, wq_ref[...]).astype(BF16)
    c = (X_HEAD_DIM ** -0.5) * LOG2E
    heads = []
    for hd in range(X_HEADS):
        cols = slice(X_HEAD_DIM * hd, X_HEAD_DIM * (hd + 1))
        s = _dot_nt(q[:, cols], k_ref[:, cols])
        m = jnp.max(s, axis=1, keepdims=True)
        e = jnp.exp2((s - m) * c)
        den = jnp.sum(e, axis=1, keepdims=True)
        heads.append((_dot(e.astype(BF16), v_ref[:, cols]) / den).astype(BF16))
    o = _dot(jnp.concatenate(heads, axis=1), wo_ref[...])
    o_ref[...] = x1 + _rms(o, gpost_ref[...])


def _mix_mem(x, oa, ob, oc, wmix, gmix, gpre, wq, kv, wo, gpost, seq):
    t = x.shape[0]
    tm = TM_MEM
    per_b = seq // tm
    row = lambda w: pl.BlockSpec((tm, w), lambda i: (i, 0))
    full = lambda a: pl.BlockSpec(a.shape, lambda i: (0,) * a.ndim)
    return pl.pallas_call(
        _mix_mem_kernel,
        grid=(t // tm,),
        in_specs=[row(D_MODEL), row(MLA_HEADS * MLA_V), row(SGU_WIDTH), row(DIFF_HEADS * DIFF_V_DIM),
                  full(wmix), full(gmix), full(gpre), full(wq),
                  pl.BlockSpec((N_MEM, D_MODEL), lambda i: (i // per_b, 0)),
                  pl.BlockSpec((N_MEM, D_MODEL), lambda i: (i // per_b, 1)),
                  full(wo), full(gpost)],
        out_specs=row(D_MODEL),
        out_shape=jax.ShapeDtypeStruct((t, D_MODEL), F32),
        compiler_params=_params("parallel"),
        name="mix_mem",
    )(x, oa, ob, oc, wmix, gmix, gpre, wq, kv, kv, wo, gpost)


def _ffn_up_kernel(x_ref, xp_ref, xn_ref, g_ref, wg_ref, wu_ref, cwg_ref, cwu_ref, cbg_ref, cbu_ref,
                   o_ref, h_ref, *, seq):
    tm = x_ref.shape[0]
    i = pl.program_id(0)

    @pl.when(pl.program_id(1) == 0)
    def _():
        g = g_ref[...]
        at_start = (i * tm) % seq == 0
        at_end = ((i + 1) * tm) % seq == 0
        h_ref[0:FFN_HALO, :] = jnp.where(at_start, 0.0, _rms(xp_ref[...], g)).astype(BF16)
        h_ref[FFN_HALO:FFN_HALO + tm, :] = _rms(x_ref[...], g).astype(BF16)
        h_ref[FFN_HALO + tm:, :] = jnp.where(at_end, 0.0, _rms(xn_ref[...], g)).astype(BF16)

    hb = h_ref[...]
    rows = tm + 2 * FFN_HALO

    def conv(w_ref, cw_ref, cb_ref):
        a = _dot(hb, w_ref[...])
        cw = cw_ref[...]
        c = cw[0:1] * pltpu.roll(a, 1, 0) + cw[1:2] * a + cw[2:3] * pltpu.roll(a, rows - 1, 0)
        return c[FFN_HALO:FFN_HALO + tm] + cb_ref[...]

    gate = conv(wg_ref, cwg_ref, cbg_ref)
    up = conv(wu_ref, cwu_ref, cbu_ref)
    o_ref[...] = (_gelu(gate) * up).astype(BF16)


def _ffn_up(x, g, wup, cw, cb, seq):
    t = x.shape[0]
    tm = TM_FFN
    tn = FFN_TN
    nj = D_FF // tn
    hb = tm // FFN_HALO
    last = t // FFN_HALO - 1
    return pl.pallas_call(
        functools.partial(_ffn_up_kernel, seq=seq),
        grid=(t // tm, nj),
        in_specs=[pl.BlockSpec((tm, D_MODEL), lambda i, j: (i, 0)),
                  pl.BlockSpec((FFN_HALO, D_MODEL), lambda i, j: (jnp.maximum(i * hb - 1, 0), 0)),
                  pl.BlockSpec((FFN_HALO, D_MODEL), lambda i, j: (jnp.minimum((i + 1) * hb, last), 0)),
                  pl.BlockSpec((1, D_MODEL), lambda i, j: (0, 0)),
                  pl.BlockSpec((D_MODEL, tn), lambda i, j: (0, j)),
                  pl.BlockSpec((D_MODEL, tn), lambda i, j: (0, j + nj)),
                  pl.BlockSpec((CONV_WIDTH, tn), lambda i, j: (0, j)),
                  pl.BlockSpec((CONV_WIDTH, tn), lambda i, j: (0, j + nj)),
                  pl.BlockSpec((1, tn), lambda i, j: (0, j)),
                  pl.BlockSpec((1, tn), lambda i, j: (0, j + nj))],
        out_specs=pl.BlockSpec((tm, tn), lambda i, j: (i, j)),
        out_shape=jax.ShapeDtypeStruct((t, D_FF), BF16),
        scratch_shapes=[pltpu.VMEM((tm + 2 * FFN_HALO, D_MODEL), BF16)],
        compiler_params=_params("parallel", "arbitrary"),
        name="ffn_up",
    )(x, x, x, g, wup, wup, cw, cw, cb, cb)


def _ffn_down_kernel(x_ref, a_ref, w_ref, g_ref, o_ref):
    o_ref[...] = x_ref[...] + _rms(_dot(a_ref[...], w_ref[...]), g_ref[...])


def _ffn_down(x, a, w, g):
    t = x.shape[0]
    tm = TM_FFN
    return pl.pallas_call(
        _ffn_down_kernel,
        grid=(t // tm,),
        in_specs=[pl.BlockSpec((tm, D_MODEL), lambda i: (i, 0)),
                  pl.BlockSpec((tm, D_FF), lambda i: (i, 0)),
                  pl.BlockSpec((D_FF, D_MODEL), lambda i: (0, 0)),
                  pl.BlockSpec((1, D_MODEL), lambda i: (0, 0))],
        out_specs=pl.BlockSpec((tm, D_MODEL), lambda i: (i, 0)),
        out_shape=jax.ShapeDtypeStruct((t, D_MODEL), F32),
        compiler_params=_params("parallel"),
        name="ffn_down",
    )(x, a, w, g)


def _take_cols(w, idx):
    wz = jnp.concatenate([w, jnp.zeros(w.shape[:-1] + (1,), w.dtype)], axis=-1)
    return jnp.take(wz, jnp.asarray(np.where(idx < 0, w.shape[-1], idx)), axis=-1)


def _mla_head_lanes():
    src = np.full((HEAD_PAD,), -1, np.int64)
    src[0:16] = MLA_NOPE + np.arange(16)
    src[16:64] = np.arange(48)
    src[64:80] = MLA_NOPE + 16 + np.arange(16)
    src[80:96] = 48 + np.arange(16)
    return src


def _diff_lanes():
    src = np.zeros((256,), np.int64)
    for g in range(DIFF_GROUPS):
        base = g * DIFF_QK_DIM
        for i in range(DIFF_ROT):
            src[g * 8 + i] = base + i
            src[128 + g * 8 + i] = base + DIFF_ROT + i
            src[64 + g * 8 + i] = base + 2 * DIFF_ROT + i
            src[192 + g * 8 + i] = base + 3 * DIFF_ROT + i
    return src


def _layouts():
    head = _mla_head_lanes()
    p_cq, p_ckv, p_kr, p_sgu = MLA_Q_LORA, MLA_KV_LORA, MLA_ROPE, 2 * SGU_WIDTH
    o_kr = p_cq + p_ckv
    o_z = o_kr + p_kr
    o_dq = o_z + p_sgu
    o_dk = o_dq + 256
    o_dv = o_dk + 256
    kr_src = np.where(head >= MLA_NOPE, o_kr + head - MLA_NOPE, -1)
    dl = _diff_lanes()
    win_idx = np.concatenate([np.arange(0, o_kr), kr_src, np.arange(o_z, o_dq), o_dq + dl, o_dk + dl,
                              np.arange(o_dv, o_dv + 256)])
    assert win_idx.shape[0] == IN_PAD
    per_q = MLA_NOPE + MLA_ROPE
    wuq_idx = np.concatenate([np.where(head >= 0, h * per_q + head, -1) for h in range(MLA_HEADS)])
    per_kv = MLA_NOPE + MLA_V
    k_src = np.where((head >= 0) & (head < MLA_NOPE), head, -1)
    wuk_idx = np.concatenate([np.where(k_src >= 0, h * per_kv + k_src, -1) for h in range(MLA_HEADS)])
    wuv_idx = np.concatenate([h * per_kv + MLA_NOPE + np.arange(MLA_V) for h in range(MLA_HEADS)])
    inv_a = ROPE_THETA ** (-jnp.arange(0, MLA_ROPE, 2, dtype=F32) / MLA_ROPE)
    inv_d = ROPE_THETA ** (-jnp.arange(0, 2 * DIFF_ROT, 2, dtype=F32) / (2 * DIFF_ROT))
    fa = jnp.zeros((LANES,), F32).at[0:16].set(inv_a).at[64:80].set(inv_a)
    ga = jnp.zeros((LANES,), F32).at[0:16].set(-1.0).at[64:80].set(1.0)
    fd = jnp.zeros((256,), F32).at[0:64].set(jnp.tile(inv_d, 8)).at[128:192].set(jnp.tile(inv_d, 8))
    gd = jnp.zeros((256,), F32).at[0:64].set(-1.0).at[128:192].set(1.0)
    return win_idx, wuq_idx, wuk_idx, wuv_idx, fa[None], ga[None], fd[None], gd[None]


def kernel(x, mem, positions, mix_pre_g, mix_post_g, w_in, mla_cq_g, mla_ckv_g, mla_w_uq, mla_w_ukv,
           sgu_norm_g, sgu_w_s, sgu_b_s, diff_lam_q1, diff_lam_k1, diff_lam_q2, diff_lam_k2, diff_sub_g,
           w_mix_out, mem_pre_g, mem_post_g, mem_kv_g, mem_w_q, mem_w_kv, mem_w_o,
           ffn_pre_g, ffn_post_g, ffn_w_up, ffn_conv_w, ffn_conv_b, ffn_w_down):
    batch, seq, d = x.shape
    t = batch * seq
    assert d == D_MODEL and seq % TM_FFN == 0 and seq % TQ == 0 and seq % SGU_CHUNK == 0
    win_idx, wuq_idx, wuk_idx, wuv_idx, fa, ga, fd, gd = _layouts()

    ca, sa, cd, sd = _rope_tables(positions.reshape(t, 1), fa, ga, fd, gd)

    vec = lambda a: a.astype(F32)[:, None, :]
    lam_init = np.array([0.8 - 0.6 * math.exp(-0.3 * l) for l in range(DEPTH)], np.float32)
    layers = dict(
        g_mix_pre=vec(mix_pre_g), g_mix_post=vec(mix_post_g),
        w_in=_take_cols(w_in, win_idx).astype(BF16),
        g_cq=vec(mla_cq_g), g_ckv=vec(mla_ckv_g),
        w_uq=_take_cols(mla_w_uq, wuq_idx).astype(BF16),
        w_uk=_take_cols(mla_w_ukv, wuk_idx).astype(BF16),
        w_uv=_take_cols(mla_w_ukv, wuv_idx).astype(BF16),
        g_sgu=vec(sgu_norm_g.reshape(DEPTH, SGU_WIDTH)),
        w_s=sgu_w_s.astype(BF16),
        b_s=jnp.repeat(jnp.swapaxes(sgu_b_s.astype(F32), 1, 2), SGU_GROUP_DIM, axis=2),
        lam=jnp.stack([diff_lam_q1, diff_lam_k1, diff_lam_q2, diff_lam_k2], axis=1).astype(F32),
        lam_init=jnp.asarray(lam_init).reshape(DEPTH, 1, 1),
        g_sub=vec(jnp.tile(diff_sub_g, (1, 2))),
        w_mix=w_mix_out.astype(BF16),
        g_mem_pre=vec(mem_pre_g), g_mem_post=vec(mem_post_g), g_mem_kv=vec(mem_kv_g),
        w_mq=mem_w_q.astype(BF16), w_mkv=mem_w_kv.astype(BF16), w_mo=mem_w_o.astype(BF16),
        g_ffn_pre=vec(ffn_pre_g), g_ffn_post=vec(ffn_post_g),
        w_up=ffn_w_up.astype(BF16), conv_w=ffn_conv_w.astype(F32), conv_b=vec(ffn_conv_b),
        w_down=ffn_w_down.astype(BF16),
    )
    mem2 = mem.reshape(batch * N_MEM, D_MODEL)

    def layer(xc, p):
        qa, ka, va, sgu, qd, kd, vd = _mix_in(
            xc, p["g_mix_pre"], p["w_in"], p["g_cq"], p["g_ckv"], p["w_uq"], p["w_uk"], p["w_uv"],
            ca, sa, cd, sd, p["g_sgu"], p["w_s"], p["b_s"])
        out_a = _mla_attn(qa, ka, va, batch, seq)
        out_c = _diff_attn(p["lam"], p["lam_init"], p["g_sub"], qd, kd, vd, batch, seq)
        kv = _mem_kv(mem2, p["g_mem_kv"], p["w_mkv"])
        x2 = _mix_mem(xc, out_a, sgu, out_c, p["w_mix"], p["g_mix_post"], p["g_mem_pre"], p["w_mq"],
                      kv, p["w_mo"], p["g_mem_post"], seq)
        mid = _ffn_up(x2, p["g_ffn_pre"], p["w_up"], p["conv_w"], p["conv_b"], seq)
        x3 = _ffn_down(x2, mid, p["w_down"], p["g_ffn_post"])
        return x3, None

    out, _ = lax.scan(layer, x.reshape(t, D_MODEL), layers)
    return out.reshape(batch, seq, D_MODEL)
```

```python
import functools
import math

import jax
import jax.numpy as jnp
import numpy as np
from jax import lax
from jax.experimental import pallas as pl
from jax.experimental.pallas import tpu as pltpu

F32 = jnp.float32
BF16 = jnp.bfloat16

D_MODEL = 1024
DEPTH = 4
N_MEM = 256
ROPE_THETA = 500000.0
EPS = 1e-6

MLA_HEADS = 8
MLA_NOPE = 64
MLA_ROPE = 32
MLA_V = 64
MLA_Q_LORA = 384
MLA_KV_LORA = 256

SGU_GROUPS = 4
SGU_GROUP_DIM = 64
SGU_WIDTH = SGU_GROUPS * SGU_GROUP_DIM
SGU_CHUNK = 128

DIFF_HEADS = 4
DIFF_QK_DIM = 32
DIFF_V_DIM = 64
DIFF_ROT = 8
DIFF_GROUPS = 2 * DIFF_HEADS

X_HEADS = 4
X_HEAD_DIM = D_MODEL // X_HEADS

D_FF = 2816
CONV_WIDTH = 3

LANES = 128
HEAD_PAD = LANES
VT_ROWS = MLA_V + 16
LOG2E = math.log2(math.e)
Q_SCALE_MLA = (MLA_NOPE + MLA_ROPE) ** -0.5 * LOG2E
Q_SCALE_DIFF = DIFF_QK_DIM ** -0.5 * LOG2E
VMEM_LIMIT = 56 * 1024 * 1024

O_CQ = 0
O_CKV = O_CQ + MLA_Q_LORA
O_KR = O_CKV + MLA_KV_LORA
O_Z = O_KR + HEAD_PAD
O_DQ = O_Z + 2 * SGU_WIDTH
O_DK = O_DQ + 256
O_DV = O_DK + 256
IN_PAD = O_DV + 256

TM_IN = 256
TQ = 256
KV_CHUNK = 256
ATTN_AHEAD = 6
TM_MEM = 512
TM_FFN = 512
FFN_HALO = 16
FFN_TN = D_FF // 2


def _call(body, name, grid, in_specs, out_specs, out_shape, semantics, scratch=()):
    return pl.pallas_call(
        body,
        grid_spec=pltpu.PrefetchScalarGridSpec(
            num_scalar_prefetch=1, grid=grid, in_specs=in_specs, out_specs=out_specs,
            scratch_shapes=list(scratch)),
        out_shape=out_shape,
        compiler_params=pltpu.CompilerParams(dimension_semantics=semantics, vmem_limit_bytes=VMEM_LIMIT),
        name=name)


def _rows(tm, width):
    return pl.BlockSpec((tm, width), lambda i, l: (i, 0))


def _layer(*tail):
    return pl.BlockSpec((None,) + tail, lambda *a: (a[-1][0],) + (0,) * len(tail))


def _rms(x, g):
    ms = jnp.mean(x * x, axis=-1, keepdims=True)
    return x * lax.rsqrt(ms + EPS) * g


def _gelu(x):
    c = math.sqrt(2.0 / math.pi)
    return 0.5 * x * (1.0 + jnp.tanh(c * (x + 0.044715 * (x * x * x))))


def _dot(a, b):
    return jnp.dot(a, b, preferred_element_type=F32)


def _dot_nt(a, b):
    return lax.dot_general(a, b, (((1,), (1,)), ((), ())), preferred_element_type=F32)


def _rope_kernel(pos_ref, fa_ref, ga_ref, fd_ref, gd_ref, ca_ref, sa_ref, cd_ref, sd_ref):
    pos = pos_ref[...].astype(F32)
    ang_a = pos * fa_ref[...]
    ca_ref[...] = jnp.cos(ang_a)
    sa_ref[...] = jnp.sin(ang_a) * ga_ref[...]
    ang_d = pos * fd_ref[...]
    cd_ref[...] = jnp.cos(ang_d)
    sd_ref[...] = jnp.sin(ang_d) * gd_ref[...]


def _rope_tables(pos, fa, ga, fd, gd):
    t = pos.shape[0]
    tm = 1024
    row = lambda w: pl.BlockSpec((tm, w), lambda i: (i, 0))
    vec = lambda w: pl.BlockSpec((1, w), lambda i: (0, 0))
    return pl.pallas_call(
        _rope_kernel,
        grid=(t // tm,),
        in_specs=[row(1), vec(LANES), vec(LANES), vec(256), vec(256)],
        out_specs=[row(LANES), row(LANES), row(256), row(256)],
        out_shape=[jax.ShapeDtypeStruct((t, LANES), F32), jax.ShapeDtypeStruct((t, LANES), F32),
                   jax.ShapeDtypeStruct((t, 256), F32), jax.ShapeDtypeStruct((t, 256), F32)],
        compiler_params=pltpu.CompilerParams(dimension_semantics=("parallel",), vmem_limit_bytes=VMEM_LIMIT),
        name="rope_tables",
    )(pos, fa, ga, fd, gd)


def _mem_kv_kernel(mem_ref, g_ref, w_ref, o_ref):
    hn = _rms(mem_ref[...], g_ref[...]).astype(BF16)
    o_ref[...] = _dot(hn, w_ref[...].astype(BF16)).astype(BF16)


def _mem_kv(mem2, g, w):
    rows = mem2.shape[0]
    nb = rows // N_MEM
    return pl.pallas_call(
        _mem_kv_kernel,
        grid=(DEPTH, nb),
        in_specs=[pl.BlockSpec((N_MEM, D_MODEL), lambda l, i: (i, 0)),
                  pl.BlockSpec((None, 1, D_MODEL), lambda l, i: (l, 0, 0)),
                  pl.BlockSpec((None, D_MODEL, 2 * D_MODEL), lambda l, i: (l, 0, 0))],
        out_specs=pl.BlockSpec((None, N_MEM, 2 * D_MODEL), lambda l, i: (l, i, 0)),
        out_shape=jax.ShapeDtypeStruct((DEPTH, rows, 2 * D_MODEL), BF16),
        compiler_params=pltpu.CompilerParams(dimension_semantics=("parallel", "parallel"),
                                             vmem_limit_bytes=VMEM_LIMIT),
        name="mem_kv",
    )(mem2, g, w)


def _mix_in_kernel(_, x_ref, g_ref, win_ref, gcq_ref, gckv_ref, wuq_ref, wuk_ref, wuv_ref,
                   ca_ref, sa_ref, cd_ref, sd_ref, gsgu_ref, ws_ref, bs_ref,
                   qa_ref, ka_ref, va_ref, sgu_ref, qd_ref, kd_ref, vd_ref):
    tm = x_ref.shape[0]
    h = _rms(x_ref[...], g_ref[...]).astype(BF16)
    proj = _dot(h, win_ref[...])

    lane = lax.broadcasted_iota(jnp.int32, (tm, LANES), 1)
    low_half = lane < 64

    ca = ca_ref[...]
    sa = sa_ref[...]
    cqn = _rms(proj[:, O_CQ:O_CQ + MLA_Q_LORA], gcq_ref[...]).astype(BF16)
    q = _dot(cqn, wuq_ref[...])
    ckvn = _rms(proj[:, O_CKV:O_CKV + MLA_KV_LORA], gckv_ref[...]).astype(BF16)
    kn = _dot(ckvn, wuk_ref[...])
    vv = _dot(ckvn, wuv_ref[...])
    kr = proj[:, O_KR:O_KR + HEAD_PAD]
    kr = kr * ca + pltpu.roll(kr, 64, 1) * sa
    vv_t = vv.T
    ones_row = (lax.broadcasted_iota(jnp.int32, (VT_ROWS - MLA_V, tm), 0) == 0).astype(BF16)
    for hd in range(MLA_HEADS):
        qh = q[:, HEAD_PAD * hd:HEAD_PAD * (hd + 1)]
        qa_ref[hd] = ((qh * ca + pltpu.roll(qh, 64, 1) * sa) * Q_SCALE_MLA).astype(BF16)
        ka_ref[hd] = (kn[:, HEAD_PAD * hd:HEAD_PAD * (hd + 1)] + kr).astype(BF16)
        va_ref[hd, 0:MLA_V, :] = vv_t[MLA_V * hd:MLA_V * (hd + 1), :].astype(BF16)
        va_ref[hd, MLA_V:VT_ROWS, :] = ones_row

    zg = _gelu(proj[:, O_Z:O_Z + 2 * SGU_WIDTH])
    u = zg[:, :SGU_WIDTH]
    gsgu = gsgu_ref[...]
    inv_gd = 1.0 / SGU_GROUP_DIM
    vn_tiles = []
    for t in range(SGU_WIDTH // LANES):
        vt = zg[:, SGU_WIDTH + LANES * t:SGU_WIDTH + LANES * (t + 1)]
        s_lo = jnp.sum(jnp.where(low_half, vt, 0.0), axis=1, keepdims=True)
        s_hi = jnp.sum(jnp.where(low_half, 0.0, vt), axis=1, keepdims=True)
        xc = vt - jnp.where(low_half, s_lo, s_hi) * inv_gd
        sq = xc * xc
        v_lo = jnp.sum(jnp.where(low_half, sq, 0.0), axis=1, keepdims=True)
        v_hi = jnp.sum(jnp.where(low_half, 0.0, sq), axis=1, keepdims=True)
        var = jnp.where(low_half, v_lo, v_hi) * inv_gd
        vn_tiles.append((xc * lax.rsqrt(var + EPS) * gsgu[:, LANES * t:LANES * (t + 1)]).astype(BF16))
    bias = bs_ref[...]
    low_chunk = lax.broadcasted_iota(jnp.int32, (SGU_CHUNK, LANES), 1) < 64
    for n in range(tm // SGU_CHUNK):
        rows = slice(SGU_CHUNK * n, SGU_CHUNK * (n + 1))
        for t in range(SGU_WIDTH // LANES):
            vc = vn_tiles[t][rows, :]
            m_lo = _dot(ws_ref[2 * t], vc)
            m_hi = _dot(ws_ref[2 * t + 1], vc)
            mixed = jnp.where(low_chunk, m_lo, m_hi) + bias[:, LANES * t:LANES * (t + 1)]
            sgu_ref[rows, LANES * t:LANES * (t + 1)] = (u[rows, LANES * t:LANES * (t + 1)] * mixed).astype(BF16)

    cd = cd_ref[...]
    sd = sd_ref[...]
    dq = proj[:, O_DQ:O_DQ + 256]
    qd_ref[...] = ((dq * cd + pltpu.roll(dq, LANES, 1) * sd) * Q_SCALE_DIFF).astype(BF16)
    dk = proj[:, O_DK:O_DK + 256]
    dk = dk * cd + pltpu.roll(dk, LANES, 1) * sd
    group = (lax.broadcasted_iota(jnp.int32, (tm, 256), 1) & 63) >> 3
    for g in range(DIFF_GROUPS):
        kd_ref[g] = jnp.where(group == g, dk, 0.0).astype(BF16)
    dv_t = proj[:, O_DV:O_DV + 256].T
    for hd in range(DIFF_HEADS):
        vd_ref[hd, 0:DIFF_V_DIM, :] = dv_t[DIFF_V_DIM * hd:DIFF_V_DIM * (hd + 1), :].astype(BF16)
        vd_ref[hd, DIFF_V_DIM:VT_ROWS, :] = ones_row


def _mix_in(lidx, x, p, ca, sa, cd, sd):
    t = x.shape[0]
    tm = TM_IN
    heads = lambda n, w: pl.BlockSpec((n, tm, w), lambda i, l: (0, i, 0))
    return _call(
        _mix_in_kernel, "mix_in", (t // tm,),
        in_specs=[_rows(tm, D_MODEL), _layer(1, D_MODEL), _layer(D_MODEL, IN_PAD),
                  _layer(1, MLA_Q_LORA), _layer(1, MLA_KV_LORA),
                  _layer(MLA_Q_LORA, MLA_HEADS * HEAD_PAD), _layer(MLA_KV_LORA, MLA_HEADS * HEAD_PAD),
                  _layer(MLA_KV_LORA, MLA_HEADS * MLA_V),
                  _rows(tm, LANES), _rows(tm, LANES), _rows(tm, 256), _rows(tm, 256),
                  _layer(1, SGU_WIDTH), _layer(SGU_GROUPS, SGU_CHUNK, SGU_CHUNK), _layer(SGU_CHUNK, SGU_WIDTH)],
        out_specs=[heads(MLA_HEADS, HEAD_PAD), heads(MLA_HEADS, HEAD_PAD),
                   pl.BlockSpec((MLA_HEADS, VT_ROWS, tm), lambda i, l: (0, 0, i)),
                   _rows(tm, SGU_WIDTH), _rows(tm, 256), heads(DIFF_GROUPS, 256),
                   pl.BlockSpec((DIFF_HEADS, VT_ROWS, tm), lambda i, l: (0, 0, i))],
        out_shape=[jax.ShapeDtypeStruct((MLA_HEADS, t, HEAD_PAD), BF16),
                   jax.ShapeDtypeStruct((MLA_HEADS, t, HEAD_PAD), BF16),
                   jax.ShapeDtypeStruct((MLA_HEADS, VT_ROWS, t), BF16),
                   jax.ShapeDtypeStruct((t, SGU_WIDTH), BF16),
                   jax.ShapeDtypeStruct((t, 256), BF16),
                   jax.ShapeDtypeStruct((DIFF_GROUPS, t, 256), BF16),
                   jax.ShapeDtypeStruct((DIFF_HEADS, VT_ROWS, t), BF16)],
        semantics=("parallel",),
    )(lidx, x, p["g_mix_pre"], p["w_in"], p["g_cq"], p["g_ckv"], p["w_uq"], p["w_uk"], p["w_uv"],
      ca, sa, cd, sd, p["g_sgu"], p["w_s"], p["b_s"])


def _attend_t(maps, seq, dv):
    n_chunks = seq // KV_CHUNK
    items = [(i, c) for i in range(len(maps)) for c in range(n_chunks)]
    scores = lambda item: _dot_nt(maps[item[0]][0](item[1]), maps[item[0]][1])
    pending = [scores(item) for item in items[:ATTN_AHEAD]]
    outs = []
    m = acc = None
    for n, (i, c) in enumerate(items):
        if n + ATTN_AHEAD < len(items):
            pending.append(scores(items[n + ATTN_AHEAD]))
        s = pending.pop(0)
        mc = jnp.max(s, axis=0, keepdims=True)
        m_new = mc if c == 0 else jnp.maximum(m, mc)
        e = jnp.exp2(s - m_new).astype(BF16)
        r = _dot(maps[i][2](c), e)
        acc = r if c == 0 else acc * jnp.exp2(m - m_new) + r
        m = m_new
        if c == n_chunks - 1:
            outs.append(acc[0:dv] / acc[dv:dv + 1])
    return outs


def _chunk(c):
    return pl.ds(c * KV_CHUNK, KV_CHUNK)


def _mla_attn_kernel(_, q_ref, k_ref, v_ref, o_ref):
    seq = k_ref.shape[1]
    maps = [(lambda c, hd=hd: k_ref[hd, _chunk(c), :], q_ref[hd], lambda c, hd=hd: v_ref[hd, :, _chunk(c)])
            for hd in range(MLA_HEADS)]
    outs = _attend_t(maps, seq, MLA_V)
    for pair in range(MLA_HEADS // 2):
        o_t = jnp.concatenate(outs[2 * pair:2 * pair + 2], axis=0)
        o_ref[:, LANES * pair:LANES * (pair + 1)] = o_t.T.astype(BF16)


def _mla_attn(lidx, qa, ka, va, batch, seq):
    t = qa.shape[1]
    nq = seq // TQ
    return _call(
        _mla_attn_kernel, "mla_attn", (batch, nq),
        in_specs=[pl.BlockSpec((MLA_HEADS, TQ, HEAD_PAD), lambda b, i, l: (0, b * nq + i, 0)),
                  pl.BlockSpec((MLA_HEADS, seq, HEAD_PAD), lambda b, i, l: (0, b, 0)),
                  pl.BlockSpec((MLA_HEADS, VT_ROWS, seq), lambda b, i, l: (0, 0, b))],
        out_specs=pl.BlockSpec((TQ, MLA_HEADS * MLA_V), lambda b, i, l: (b * nq + i, 0)),
        out_shape=jax.ShapeDtypeStruct((t, MLA_HEADS * MLA_V), BF16),
        semantics=("parallel", "arbitrary"),
    )(lidx, qa, ka, va)


def _diff_attn_kernel(_, lam_ref, linit_ref, gsub_ref, q_ref, k_ref, v_ref, o_ref):
    lam_init = linit_ref[...]
    lam = (jnp.exp(jnp.sum(lam_ref[0:1, :] * lam_ref[1:2, :], axis=1, keepdims=True))
           - jnp.exp(jnp.sum(lam_ref[2:3, :] * lam_ref[3:4, :], axis=1, keepdims=True)) + lam_init)
    gsub = gsub_ref[...] * (1.0 - lam_init)
    gsub = jnp.concatenate([gsub] * (TQ // LANES), axis=1)
    seq = k_ref.shape[1]
    q = q_ref[...]
    maps = [(lambda c, g=g: k_ref[g, _chunk(c), :], q, lambda c, g=g: v_ref[g // 2, :, _chunk(c)])
            for g in range(DIFF_GROUPS)]
    outs = _attend_t(maps, seq, DIFF_V_DIM)
    normed = []
    for hd in range(DIFF_HEADS):
        o = outs[2 * hd] - lam * outs[2 * hd + 1]
        ms = jnp.mean(o * o, axis=0, keepdims=True)
        normed.append(o * lax.rsqrt(ms + EPS) * gsub)
    for pair in range(DIFF_HEADS // 2):
        o_t = jnp.concatenate(normed[2 * pair:2 * pair + 2], axis=0)
        o_ref[:, LANES * pair:LANES * (pair + 1)] = o_t.T.astype(BF16)


def _diff_attn(lidx, p, qd, kd, vd, batch, seq):
    t = qd.shape[0]
    nq = seq // TQ
    return _call(
        _diff_attn_kernel, "diff_attn", (batch, nq),
        in_specs=[_layer(4, DIFF_QK_DIM), _layer(1, 1), _layer(DIFF_V_DIM, LANES),
                  pl.BlockSpec((TQ, 256), lambda b, i, l: (b * nq + i, 0)),
                  pl.BlockSpec((DIFF_GROUPS, seq, 256), lambda b, i, l: (0, b, 0)),
                  pl.BlockSpec((DIFF_HEADS, VT_ROWS, seq), lambda b, i, l: (0, 0, b))],
        out_specs=pl.BlockSpec((TQ, 256), lambda b, i, l: (b * nq + i, 0)),
        out_shape=jax.ShapeDtypeStruct((t, DIFF_HEADS * DIFF_V_DIM), BF16),
        semantics=("parallel", "arbitrary"),
    )(lidx, p["lam"], p["lam_init"], p["g_sub"], qd, kd, vd)


def _mix_mem_kernel(_, x_ref, oa_ref, ob_ref, oc_ref, wmix_ref, gmix_ref, gpre_ref, wq_ref,
                    k_ref, v_ref, wo_ref, gpost_ref, o_ref):
    cat = jnp.concatenate([oa_ref[...], ob_ref[...], oc_ref[...]], axis=1)
    x1 = x_ref[...] + _rms(_dot(cat, wmix_ref[...]), gmix_ref[...])
    hq = _rms(x1, gpre_ref[...]).astype(BF16)
    q = _dot(hq, wq_ref[...]).astype(BF16)
    c = (X_HEAD_DIM ** -0.5) * LOG2E
    heads = []
    for hd in range(X_HEADS):
        cols = slice(X_HEAD_DIM * hd, X_HEAD_DIM * (hd + 1))
        s = _dot_nt(q[:, cols], k_ref[:, cols])
        m = jnp.max(s, axis=1, keepdims=True)
        e = jnp.exp2((s - m) * c)
        den = jnp.sum(e, axis=1, keepdims=True)
        heads.append((_dot(e.astype(BF16), v_ref[:, cols]) / den).astype(BF16))
    o = _dot(jnp.concatenate(heads, axis=1), wo_ref[...])
    o_ref[...] = x1 + _rms(o, gpost_ref[...])


def _mix_mem(lidx, x, oa, ob, oc, p, kv, seq):
    t = x.shape[0]
    tm = TM_MEM
    per_b = seq // tm
    return _call(
        _mix_mem_kernel, "mix_mem", (t // tm,),
        in_specs=[_rows(tm, D_MODEL), _rows(tm, MLA_HEADS * MLA_V), _rows(tm, SGU_WIDTH),
                  _rows(tm, DIFF_HEADS * DIFF_V_DIM),
                  _layer(D_MODEL, D_MODEL), _layer(1, D_MODEL), _layer(1, D_MODEL), _layer(D_MODEL, D_MODEL),
                  pl.BlockSpec((None, N_MEM, D_MODEL), lambda i, l: (l[0], i // per_b, 0)),
                  pl.BlockSpec((None, N_MEM, D_MODEL), lambda i, l: (l[0], i // per_b, 1)),
                  _layer(D_MODEL, D_MODEL), _layer(1, D_MODEL)],
        out_specs=_rows(tm, D_MODEL),
        out_shape=jax.ShapeDtypeStruct((t, D_MODEL), F32),
        semantics=("parallel",),
    )(lidx, x, oa, ob, oc, p["w_mix"], p["g_mix_post"], p["g_mem_pre"], p["w_mq"], kv, kv,
      p["w_mo"], p["g_mem_post"])


def _ffn_up_kernel(_, x_ref, xp_ref, xn_ref, g_ref, wg_ref, wu_ref, cwg_ref, cwu_ref, cbg_ref, cbu_ref,
                   o_ref, h_ref, *, seq):
    tm = x_ref.shape[0]
    i = pl.program_id(0)

    @pl.when(pl.program_id(1) == 0)
    def _():
        g = g_ref[...]
        at_start = (i * tm) % seq == 0
        at_end = ((i + 1) * tm) % seq == 0
        h_ref[0:FFN_HALO, :] = jnp.where(at_start, 0.0, _rms(xp_ref[...], g)).astype(BF16)
        h_ref[FFN_HALO:FFN_HALO + tm, :] = _rms(x_ref[...], g).astype(BF16)
        h_ref[FFN_HALO + tm:, :] = jnp.where(at_end, 0.0, _rms(xn_ref[...], g)).astype(BF16)

    hb = h_ref[...]
    rows = tm + 2 * FFN_HALO

    def conv(w_ref, cw_ref, cb_ref):
        a = _dot(hb, w_ref[...])
        cw = cw_ref[...]
        c = cw[0:1] * pltpu.roll(a, 1, 0) + cw[1:2] * a + cw[2:3] * pltpu.roll(a, rows - 1, 0)
        return c[FFN_HALO:FFN_HALO + tm] + cb_ref[...]

    gate = conv(wg_ref, cwg_ref, cbg_ref)
    up = conv(wu_ref, cwu_ref, cbu_ref)
    o_ref[...] = (_gelu(gate) * up).astype(BF16)


def _ffn_up(lidx, x, p, seq):
    t = x.shape[0]
    tm = TM_FFN
    tn = FFN_TN
    nj = D_FF // tn
    hb = tm // FFN_HALO
    last = t // FFN_HALO - 1
    return _call(
        functools.partial(_ffn_up_kernel, seq=seq), "ffn_up", (t // tm, nj),
        in_specs=[pl.BlockSpec((tm, D_MODEL), lambda i, j, l: (i, 0)),
                  pl.BlockSpec((FFN_HALO, D_MODEL), lambda i, j, l: (jnp.maximum(i * hb - 1, 0), 0)),
                  pl.BlockSpec((FFN_HALO, D_MODEL), lambda i, j, l: (jnp.minimum((i + 1) * hb, last), 0)),
                  _layer(1, D_MODEL),
                  pl.BlockSpec((None, D_MODEL, tn), lambda i, j, l: (l[0], 0, j)),
                  pl.BlockSpec((None, D_MODEL, tn), lambda i, j, l: (l[0], 0, j + nj)),
                  pl.BlockSpec((None, CONV_WIDTH, tn), lambda i, j, l: (l[0], 0, j)),
                  pl.BlockSpec((None, CONV_WIDTH, tn), lambda i, j, l: (l[0], 0, j + nj)),
                  pl.BlockSpec((None, 1, tn), lambda i, j, l: (l[0], 0, j)),
                  pl.BlockSpec((None, 1, tn), lambda i, j, l: (l[0], 0, j + nj))],
        out_specs=pl.BlockSpec((tm, tn), lambda i, j, l: (i, j)),
        out_shape=jax.ShapeDtypeStruct((t, D_FF), BF16),
        semantics=("parallel", "arbitrary"),
        scratch=[pltpu.VMEM((tm + 2 * FFN_HALO, D_MODEL), BF16)],
    )(lidx, x, x, x, p["g_ffn_pre"], p["w_up"], p["w_up"], p["conv_w"], p["conv_w"], p["conv_b"], p["conv_b"])


def _ffn_down_kernel(_, x_ref, a_ref, w_ref, g_ref, o_ref):
    o_ref[...] = x_ref[...] + _rms(_dot(a_ref[...], w_ref[...]), g_ref[...])


def _ffn_down(lidx, x, a, p):
    t = x.shape[0]
    tm = TM_FFN
    return _call(
        _ffn_down_kernel, "ffn_down", (t // tm,),
        in_specs=[_rows(tm, D_MODEL), _rows(tm, D_FF), _layer(D_FF, D_MODEL), _layer(1, D_MODEL)],
        out_specs=_rows(tm, D_MODEL),
        out_shape=jax.ShapeDtypeStruct((t, D_MODEL), F32),
        semantics=("parallel",),
    )(lidx, x, a, p["w_down"], p["g_ffn_post"])


def _take_cols(w, idx):
    wz = jnp.concatenate([w, jnp.zeros(w.shape[:-1] + (1,), w.dtype)], axis=-1)
    return jnp.take(wz, jnp.asarray(np.where(idx < 0, w.shape[-1], idx)), axis=-1)


def _mla_head_lanes():
    src = np.full((HEAD_PAD,), -1, np.int64)
    src[0:16] = MLA_NOPE + np.arange(16)
    src[16:64] = np.arange(48)
    src[64:80] = MLA_NOPE + 16 + np.arange(16)
    src[80:96] = 48 + np.arange(16)
    return src


def _diff_lanes():
    src = np.zeros((256,), np.int64)
    for g in range(DIFF_GROUPS):
        base = g * DIFF_QK_DIM
        for i in range(DIFF_ROT):
            src[g * 8 + i] = base + i
            src[128 + g * 8 + i] = base + DIFF_ROT + i
            src[64 + g * 8 + i] = base + 2 * DIFF_ROT + i
            src[192 + g * 8 + i] = base + 3 * DIFF_ROT + i
    return src


def _layouts():
    head = _mla_head_lanes()
    p_cq, p_ckv, p_kr, p_sgu = MLA_Q_LORA, MLA_KV_LORA, MLA_ROPE, 2 * SGU_WIDTH
    o_kr = p_cq + p_ckv
    o_z = o_kr + p_kr
    o_dq = o_z + p_sgu
    o_dk = o_dq + 256
    o_dv = o_dk + 256
    kr_src = np.where(head >= MLA_NOPE, o_kr + head - MLA_NOPE, -1)
    dl = _diff_lanes()
    win_idx = np.concatenate([np.arange(0, o_kr), kr_src, np.arange(o_z, o_dq), o_dq + dl, o_dk + dl,
                              np.arange(o_dv, o_dv + 256)])
    assert win_idx.shape[0] == IN_PAD
    per_q = MLA_NOPE + MLA_ROPE
    wuq_idx = np.concatenate([np.where(head >= 0, h * per_q + head, -1) for h in range(MLA_HEADS)])
    per_kv = MLA_NOPE + MLA_V
    k_src = np.where((head >= 0) & (head < MLA_NOPE), head, -1)
    wuk_idx = np.concatenate([np.where(k_src >= 0, h * per_kv + k_src, -1) for h in range(MLA_HEADS)])
    wuv_idx = np.concatenate([h * per_kv + MLA_NOPE + np.arange(MLA_V) for h in range(MLA_HEADS)])
    inv_a = ROPE_THETA ** (-jnp.arange(0, MLA_ROPE, 2, dtype=F32) / MLA_ROPE)
    inv_d = ROPE_THETA ** (-jnp.arange(0, 2 * DIFF_ROT, 2, dtype=F32) / (2 * DIFF_ROT))
    fa = jnp.zeros((LANES,), F32).at[0:16].set(inv_a).at[64:80].set(inv_a)
    ga = jnp.zeros((LANES,), F32).at[0:16].set(-1.0).at[64:80].set(1.0)
    fd = jnp.zeros((256,), F32).at[0:64].set(jnp.tile(inv_d, 8)).at[128:192].set(jnp.tile(inv_d, 8))
    gd = jnp.zeros((256,), F32).at[0:64].set(-1.0).at[128:192].set(1.0)
    return win_idx, wuq_idx, wuk_idx, wuv_idx, fa[None], ga[None], fd[None], gd[None]


def kernel(x, mem, positions, mix_pre_g, mix_post_g, w_in, mla_cq_g, mla_ckv_g, mla_w_uq, mla_w_ukv,
           sgu_norm_g, sgu_w_s, sgu_b_s, diff_lam_q1, diff_lam_k1, diff_lam_q2, diff_lam_k2, diff_sub_g,
           w_mix_out, mem_pre_g, mem_post_g, mem_kv_g, mem_w_q, mem_w_kv, mem_w_o,
           ffn_pre_g, ffn_post_g, ffn_w_up, ffn_conv_w, ffn_conv_b, ffn_w_down):
    batch, seq, d = x.shape
    t = batch * seq
    assert d == D_MODEL and seq % TM_FFN == 0 and seq % TQ == 0 and seq % SGU_CHUNK == 0
    win_idx, wuq_idx, wuk_idx, wuv_idx, fa, ga, fd, gd = _layouts()

    ca, sa, cd, sd = _rope_tables(positions.reshape(t, 1), fa, ga, fd, gd)

    vec = lambda a: a.astype(F32)[:, None, :]
    lam_init = np.array([0.8 - 0.6 * math.exp(-0.3 * l) for l in range(DEPTH)], np.float32)
    w_ukv16 = mla_w_ukv.astype(BF16)
    p = dict(
        g_mix_pre=vec(mix_pre_g), g_mix_post=vec(mix_post_g),
        w_in=_take_cols(w_in.astype(BF16), win_idx),
        g_cq=vec(mla_cq_g), g_ckv=vec(mla_ckv_g),
        w_uq=_take_cols(mla_w_uq.astype(BF16), wuq_idx),
        w_uk=_take_cols(w_ukv16, wuk_idx),
        w_uv=_take_cols(w_ukv16, wuv_idx),
        g_sgu=vec(sgu_norm_g.reshape(DEPTH, SGU_WIDTH)),
        w_s=sgu_w_s.astype(BF16),
        b_s=jnp.repeat(jnp.swapaxes(sgu_b_s.astype(F32), 1, 2), SGU_GROUP_DIM, axis=2),
        lam=jnp.stack([diff_lam_q1, diff_lam_k1, diff_lam_q2, diff_lam_k2], axis=1).astype(F32),
        lam_init=jnp.asarray(lam_init).reshape(DEPTH, 1, 1),
        g_sub=jnp.repeat(diff_sub_g.astype(F32)[:, :, None], LANES, axis=2),
        w_mix=w_mix_out.astype(BF16),
        g_mem_pre=vec(mem_pre_g), g_mem_post=vec(mem_post_g),
        w_mq=mem_w_q.astype(BF16), w_mo=mem_w_o.astype(BF16),
        g_ffn_pre=vec(ffn_pre_g), g_ffn_post=vec(ffn_post_g),
        w_up=ffn_w_up.astype(BF16), conv_w=ffn_conv_w.astype(F32), conv_b=vec(ffn_conv_b),
        w_down=ffn_w_down.astype(BF16),
    )
    kv_all = _mem_kv(mem.reshape(batch * N_MEM, D_MODEL), vec(mem_kv_g), mem_w_kv)

    def layer(xc, lidx):
        qa, ka, va, sgu, qd, kd, vd = _mix_in(lidx, xc, p, ca, sa, cd, sd)
        out_a = _mla_attn(lidx, qa, ka, va, batch, seq)
        out_c = _diff_attn(lidx, p, qd, kd, vd, batch, seq)
        x2 = _mix_mem(lidx, xc, out_a, sgu, out_c, p, kv_all, seq)
        mid = _ffn_up(lidx, x2, p, seq)
        x3 = _ffn_down(lidx, x2, mid, p)
        return x3, None

    out, _ = lax.scan(layer, x.reshape(t, D_MODEL), jnp.arange(DEPTH, dtype=jnp.int32).reshape(DEPTH, 1))
    return out.reshape(batch, seq, D_MODEL)
```

```python
import functools
import math

import jax
import jax.numpy as jnp
import numpy as np
from jax import lax
from jax.experimental import pallas as pl
from jax.experimental.pallas import tpu as pltpu

F32 = jnp.float32
BF16 = jnp.bfloat16

D_MODEL = 1024
DEPTH = 4
N_MEM = 256
ROPE_THETA = 500000.0
EPS = 1e-6

MLA_HEADS = 8
MLA_NOPE = 64
MLA_ROPE = 32
MLA_V = 64
MLA_Q_LORA = 384
MLA_KV_LORA = 256

SGU_GROUPS = 4
SGU_GROUP_DIM = 64
SGU_WIDTH = SGU_GROUPS * SGU_GROUP_DIM
SGU_CHUNK = 128

DIFF_HEADS = 4
DIFF_QK_DIM = 32
DIFF_V_DIM = 64
DIFF_ROT = 8
DIFF_GROUPS = 2 * DIFF_HEADS

X_HEADS = 4
X_HEAD_DIM = D_MODEL // X_HEADS

D_FF = 2816
CONV_WIDTH = 3

LANES = 128
HEAD_PAD = LANES
VT_ROWS = MLA_V + 16
LOG2E = math.log2(math.e)
Q_SCALE_MLA = (MLA_NOPE + MLA_ROPE) ** -0.5 * LOG2E
Q_SCALE_DIFF = DIFF_QK_DIM ** -0.5 * LOG2E
VMEM_LIMIT = 56 * 1024 * 1024

O_CQ = 0
O_CKV = O_CQ + MLA_Q_LORA
O_KR = O_CKV + MLA_KV_LORA
O_Z = O_KR + HEAD_PAD
O_DQ = O_Z + 2 * SGU_WIDTH
O_DK = O_DQ + 256
O_DV = O_DK + 256
IN_PAD = O_DV + 256

TM_IN = 512
TQ = 256
KV_CHUNK = 256
ATTN_AHEAD = 6
TM_MEM = 512
TM_FFN = 512
FFN_HALO = 16
FFN_TN = D_FF // 2


def _call(body, name, grid, in_specs, out_specs, out_shape, semantics, scratch=()):
    return pl.pallas_call(
        body,
        grid_spec=pltpu.PrefetchScalarGridSpec(
            num_scalar_prefetch=1, grid=grid, in_specs=in_specs, out_specs=out_specs,
            scratch_shapes=list(scratch)),
        out_shape=out_shape,
        compiler_params=pltpu.CompilerParams(dimension_semantics=semantics, vmem_limit_bytes=VMEM_LIMIT),
        name=name)


def _rows(tm, width):
    return pl.BlockSpec((tm, width), lambda i, l: (i, 0))


def _layer(*tail):
    return pl.BlockSpec((None,) + tail, lambda *a: (a[-1][0],) + (0,) * len(tail))


def _rms(x, g):
    ms = jnp.mean(x * x, axis=-1, keepdims=True)
    return x * lax.rsqrt(ms + EPS) * g


def _gelu(x):
    c = math.sqrt(2.0 / math.pi)
    return 0.5 * x * (1.0 + jnp.tanh(c * (x + 0.044715 * (x * x * x))))


def _dot(a, b):
    return jnp.dot(a, b, preferred_element_type=F32)


def _dot_nt(a, b):
    return lax.dot_general(a, b, (((1,), (1,)), ((), ())), preferred_element_type=F32)


def _rope_kernel(pos_ref, fa_ref, ga_ref, fd_ref, ca_ref, sa_ref, cd_ref, sd_ref):
    pos = pos_ref[...].astype(F32)
    ang_a = pos * fa_ref[...]
    ca_ref[...] = jnp.cos(ang_a)
    sa_ref[...] = jnp.sin(ang_a) * ga_ref[...]
    ang_d = pos * fd_ref[...]
    cd_ref[...] = jnp.cos(ang_d)
    sd_ref[...] = jnp.sin(ang_d)


def _rope_tables(pos, fa, ga, fd):
    t = pos.shape[0]
    tm = 1024
    row = lambda w: pl.BlockSpec((tm, w), lambda i: (i, 0))
    vec = lambda w: pl.BlockSpec((1, w), lambda i: (0, 0))
    return pl.pallas_call(
        _rope_kernel,
        grid=(t // tm,),
        in_specs=[row(1), vec(LANES), vec(LANES), vec(LANES)],
        out_specs=[row(LANES)] * 4,
        out_shape=[jax.ShapeDtypeStruct((t, LANES), F32)] * 4,
        compiler_params=pltpu.CompilerParams(dimension_semantics=("parallel",), vmem_limit_bytes=VMEM_LIMIT),
        name="rope_tables",
    )(pos, fa, ga, fd)


def _mem_kv_kernel(mem_ref, g_ref, w_ref, o_ref):
    hn = _rms(mem_ref[...], g_ref[...]).astype(BF16)
    o_ref[...] = _dot(hn, w_ref[...].astype(BF16)).astype(BF16)


def _mem_kv(mem2, g, w):
    rows = mem2.shape[0]
    nb = rows // N_MEM
    return pl.pallas_call(
        _mem_kv_kernel,
        grid=(DEPTH, nb),
        in_specs=[pl.BlockSpec((N_MEM, D_MODEL), lambda l, i: (i, 0)),
                  pl.BlockSpec((None, 1, D_MODEL), lambda l, i: (l, 0, 0)),
                  pl.BlockSpec((None, D_MODEL, 2 * D_MODEL), lambda l, i: (l, 0, 0))],
        out_specs=pl.BlockSpec((None, N_MEM, 2 * D_MODEL), lambda l, i: (l, i, 0)),
        out_shape=jax.ShapeDtypeStruct((DEPTH, rows, 2 * D_MODEL), BF16),
        compiler_params=pltpu.CompilerParams(dimension_semantics=("parallel", "parallel"),
                                             vmem_limit_bytes=VMEM_LIMIT),
        name="mem_kv",
    )(mem2, g, w)


def _mix_in_kernel(_, x_ref, g_ref, win_ref, gcq_ref, gckv_ref, wuq_ref, wuk_ref, wuv_ref,
                   ca_ref, sa_ref, cd_ref, sd_ref, gsgu_ref, ws_ref, bs_ref,
                   qa_ref, ka_ref, va_ref, sgu_ref, qd_ref, kd_ref, vd_ref):
    tm = x_ref.shape[0]
    h = _rms(x_ref[...], g_ref[...]).astype(BF16)
    proj = _dot(h, win_ref[...])

    lane = lax.broadcasted_iota(jnp.int32, (tm, LANES), 1)
    low_half = lane < 64

    ca = ca_ref[...]
    sa = sa_ref[...]
    cqn = _rms(proj[:, O_CQ:O_CQ + MLA_Q_LORA], gcq_ref[...]).astype(BF16)
    q = _dot(cqn, wuq_ref[...])
    ckvn = _rms(proj[:, O_CKV:O_CKV + MLA_KV_LORA], gckv_ref[...]).astype(BF16)
    kn = _dot(ckvn, wuk_ref[...])
    vv = _dot(ckvn, wuv_ref[...])
    kr = proj[:, O_KR:O_KR + HEAD_PAD]
    kr = kr * ca + pltpu.roll(kr, 64, 1) * sa
    vv_t = vv.T
    ones_row = (lax.broadcasted_iota(jnp.int32, (VT_ROWS - MLA_V, tm), 0) == 0).astype(BF16)
    for hd in range(MLA_HEADS):
        qh = q[:, HEAD_PAD * hd:HEAD_PAD * (hd + 1)]
        qa_ref[hd] = ((qh * ca + pltpu.roll(qh, 64, 1) * sa) * Q_SCALE_MLA).astype(BF16)
        ka_ref[hd] = (kn[:, HEAD_PAD * hd:HEAD_PAD * (hd + 1)] + kr).astype(BF16)
        va_ref[hd, 0:MLA_V, :] = vv_t[MLA_V * hd:MLA_V * (hd + 1), :].astype(BF16)
        va_ref[hd, MLA_V:VT_ROWS, :] = ones_row

    zg = _gelu(proj[:, O_Z:O_Z + 2 * SGU_WIDTH])
    u = zg[:, :SGU_WIDTH]
    gsgu = gsgu_ref[...]
    inv_gd = 1.0 / SGU_GROUP_DIM
    vn_tiles = []
    for t in range(SGU_WIDTH // LANES):
        vt = zg[:, SGU_WIDTH + LANES * t:SGU_WIDTH + LANES * (t + 1)]
        s_lo = jnp.sum(jnp.where(low_half, vt, 0.0), axis=1, keepdims=True)
        s_hi = jnp.sum(jnp.where(low_half, 0.0, vt), axis=1, keepdims=True)
        xc = vt - jnp.where(low_half, s_lo, s_hi) * inv_gd
        sq = xc * xc
        v_lo = jnp.sum(jnp.where(low_half, sq, 0.0), axis=1, keepdims=True)
        v_hi = jnp.sum(jnp.where(low_half, 0.0, sq), axis=1, keepdims=True)
        var = jnp.where(low_half, v_lo, v_hi) * inv_gd
        vn_tiles.append((xc * lax.rsqrt(var + EPS) * gsgu[:, LANES * t:LANES * (t + 1)]).astype(BF16))
    bias = bs_ref[...]
    low_chunk = lax.broadcasted_iota(jnp.int32, (SGU_CHUNK, LANES), 1) < 64
    for n in range(tm // SGU_CHUNK):
        rows = slice(SGU_CHUNK * n, SGU_CHUNK * (n + 1))
        for t in range(SGU_WIDTH // LANES):
            vc = vn_tiles[t][rows, :]
            m_lo = _dot(ws_ref[2 * t], vc)
            m_hi = _dot(ws_ref[2 * t + 1], vc)
            mixed = jnp.where(low_chunk, m_lo, m_hi) + bias[:, LANES * t:LANES * (t + 1)]
            sgu_ref[rows, LANES * t:LANES * (t + 1)] = (u[rows, LANES * t:LANES * (t + 1)] * mixed).astype(BF16)

    cd = jnp.concatenate([cd_ref[...]] * 2, axis=1)
    sd = jnp.concatenate([-sd_ref[...], sd_ref[...]], axis=1)
    dq = proj[:, O_DQ:O_DQ + 256]
    qd_ref[...] = ((dq * cd + pltpu.roll(dq, LANES, 1) * sd) * Q_SCALE_DIFF).astype(BF16)
    dk = proj[:, O_DK:O_DK + 256]
    dk = dk * cd + pltpu.roll(dk, LANES, 1) * sd
    group = (lax.broadcasted_iota(jnp.int32, (tm, 256), 1) & 63) >> 3
    for g in range(DIFF_GROUPS):
        kd_ref[g] = jnp.where(group == g, dk, 0.0).astype(BF16)
    dv_t = proj[:, O_DV:O_DV + 256].T
    for hd in range(DIFF_HEADS):
        vd_ref[hd, 0:DIFF_V_DIM, :] = dv_t[DIFF_V_DIM * hd:DIFF_V_DIM * (hd + 1), :].astype(BF16)
        vd_ref[hd, DIFF_V_DIM:VT_ROWS, :] = ones_row


def _mix_in(lidx, x, p, ca, sa, cd, sd):
    t = x.shape[0]
    tm = TM_IN
    heads = lambda n, w: pl.BlockSpec((n, tm, w), lambda i, l: (0, i, 0))
    return _call(
        _mix_in_kernel, "mix_in", (t // tm,),
        in_specs=[_rows(tm, D_MODEL), _layer(1, D_MODEL), _layer(D_MODEL, IN_PAD),
                  _layer(1, MLA_Q_LORA), _layer(1, MLA_KV_LORA),
                  _layer(MLA_Q_LORA, MLA_HEADS * HEAD_PAD), _layer(MLA_KV_LORA, MLA_HEADS * HEAD_PAD),
                  _layer(MLA_KV_LORA, MLA_HEADS * MLA_V),
                  _rows(tm, LANES), _rows(tm, LANES), _rows(tm, LANES), _rows(tm, LANES),
                  _layer(1, SGU_WIDTH), _layer(SGU_GROUPS, SGU_CHUNK, SGU_CHUNK), _layer(SGU_CHUNK, SGU_WIDTH)],
        out_specs=[heads(MLA_HEADS, HEAD_PAD), heads(MLA_HEADS, HEAD_PAD),
                   pl.BlockSpec((MLA_HEADS, VT_ROWS, tm), lambda i, l: (0, 0, i)),
                   _rows(tm, SGU_WIDTH), _rows(tm, 256), heads(DIFF_GROUPS, 256),
                   pl.BlockSpec((DIFF_HEADS, VT_ROWS, tm), lambda i, l: (0, 0, i))],
        out_shape=[jax.ShapeDtypeStruct((MLA_HEADS, t, HEAD_PAD), BF16),
                   jax.ShapeDtypeStruct((MLA_HEADS, t, HEAD_PAD), BF16),
                   jax.ShapeDtypeStruct((MLA_HEADS, VT_ROWS, t), BF16),
                   jax.ShapeDtypeStruct((t, SGU_WIDTH), BF16),
                   jax.ShapeDtypeStruct((t, 256), BF16),
                   jax.ShapeDtypeStruct((DIFF_GROUPS, t, 256), BF16),
                   jax.ShapeDtypeStruct((DIFF_HEADS, VT_ROWS, t), BF16)],
        semantics=("parallel",),
    )(lidx, x, p["g_mix_pre"], p["w_in"], p["g_cq"], p["g_ckv"], p["w_uq"], p["w_uk"], p["w_uv"],
      ca, sa, cd, sd, p["g_sgu"], p["w_s"], p["b_s"])


def _attend_t(maps, seq, dv):
    n_chunks = seq // KV_CHUNK
    items = [(i, c) for i in range(len(maps)) for c in range(n_chunks)]
    scores = lambda item: _dot_nt(maps[item[0]][0](item[1]), maps[item[0]][1])
    pending = [scores(item) for item in items[:ATTN_AHEAD]]
    outs = []
    m = acc = None
    for n, (i, c) in enumerate(items):
        if n + ATTN_AHEAD < len(items):
            pending.append(scores(items[n + ATTN_AHEAD]))
        s = pending.pop(0)
        mc = jnp.max(s, axis=0, keepdims=True)
        m_new = mc if c == 0 else jnp.maximum(m, mc)
        e = jnp.exp2(s - m_new).astype(BF16)
        r = _dot(maps[i][2](c), e)
        acc = r if c == 0 else acc * jnp.exp2(m - m_new) + r
        m = m_new
        if c == n_chunks - 1:
            outs.append(acc[0:dv] / acc[dv:dv + 1])
    return outs


def _chunk(c):
    return pl.ds(c * KV_CHUNK, KV_CHUNK)


def _mla_attn_kernel(_, q_ref, k_ref, v_ref, o_ref):
    seq = k_ref.shape[1]
    maps = [(lambda c, hd=hd: k_ref[hd, _chunk(c), :], q_ref[hd], lambda c, hd=hd: v_ref[hd, :, _chunk(c)])
            for hd in range(MLA_HEADS)]
    outs = _attend_t(maps, seq, MLA_V)
    for pair in range(MLA_HEADS // 2):
        o_t = jnp.concatenate(outs[2 * pair:2 * pair + 2], axis=0)
        o_ref[:, LANES * pair:LANES * (pair + 1)] = o_t.T.astype(BF16)


def _mla_attn(lidx, qa, ka, va, batch, seq):
    t = qa.shape[1]
    nq = seq // TQ
    return _call(
        _mla_attn_kernel, "mla_attn", (batch, nq),
        in_specs=[pl.BlockSpec((MLA_HEADS, TQ, HEAD_PAD), lambda b, i, l: (0, b * nq + i, 0)),
                  pl.BlockSpec((MLA_HEADS, seq, HEAD_PAD), lambda b, i, l: (0, b, 0)),
                  pl.BlockSpec((MLA_HEADS, VT_ROWS, seq), lambda b, i, l: (0, 0, b))],
        out_specs=pl.BlockSpec((TQ, MLA_HEADS * MLA_V), lambda b, i, l: (b * nq + i, 0)),
        out_shape=jax.ShapeDtypeStruct((t, MLA_HEADS * MLA_V), BF16),
        semantics=("parallel", "arbitrary"),
    )(lidx, qa, ka, va)


def _diff_attn_kernel(_, lam_ref, linit_ref, gsub_ref, q_ref, k_ref, v_ref, o_ref):
    lam_init = linit_ref[...]
    lam = (jnp.exp(jnp.sum(lam_ref[0:1, :] * lam_ref[1:2, :], axis=1, keepdims=True))
           - jnp.exp(jnp.sum(lam_ref[2:3, :] * lam_ref[3:4, :], axis=1, keepdims=True)) + lam_init)
    gsub = gsub_ref[...] * (1.0 - lam_init)
    gsub = jnp.concatenate([gsub] * (TQ // LANES), axis=1)
    seq = k_ref.shape[1]
    q = q_ref[...]
    maps = [(lambda c, g=g: k_ref[g, _chunk(c), :], q, lambda c, g=g: v_ref[g // 2, :, _chunk(c)])
            for g in range(DIFF_GROUPS)]
    outs = _attend_t(maps, seq, DIFF_V_DIM)
    normed = []
    for hd in range(DIFF_HEADS):
        o = outs[2 * hd] - lam * outs[2 * hd + 1]
        ms = jnp.mean(o * o, axis=0, keepdims=True)
        normed.append(o * lax.rsqrt(ms + EPS) * gsub)
    for pair in range(DIFF_HEADS // 2):
        o_t = jnp.concatenate(normed[2 * pair:2 * pair + 2], axis=0)
        o_ref[:, LANES * pair:LANES * (pair + 1)] = o_t.T.astype(BF16)


def _diff_attn(lidx, p, qd, kd, vd, batch, seq):
    t = qd.shape[0]
    nq = seq // TQ
    return _call(
        _diff_attn_kernel, "diff_attn", (batch, nq),
        in_specs=[_layer(4, DIFF_QK_DIM), _layer(1, 1), _layer(DIFF_V_DIM, LANES),
                  pl.BlockSpec((TQ, 256), lambda b, i, l: (b * nq + i, 0)),
                  pl.BlockSpec((DIFF_GROUPS, seq, 256), lambda b, i, l: (0, b, 0)),
                  pl.BlockSpec((DIFF_HEADS, VT_ROWS, seq), lambda b, i, l: (0, 0, b))],
        out_specs=pl.BlockSpec((TQ, 256), lambda b, i, l: (b * nq + i, 0)),
        out_shape=jax.ShapeDtypeStruct((t, DIFF_HEADS * DIFF_V_DIM), BF16),
        semantics=("parallel", "arbitrary"),
    )(lidx, p["lam"], p["lam_init"], p["g_sub"], qd, kd, vd)


def _mix_mem_kernel(_, x_ref, oa_ref, ob_ref, oc_ref, wmix_ref, gmix_ref, gpre_ref, wq_ref,
                    k_ref, v_ref, wo_ref, gpost_ref, o_ref):
    cat = jnp.concatenate([oa_ref[...], ob_ref[...], oc_ref[...]], axis=1)
    x1 = x_ref[...] + _rms(_dot(cat, wmix_ref[...]), gmix_ref[...])
    hq = _rms(x1, gpre_ref[...]).astype(BF16)
    q = _dot(hq, wq_ref[...]).astype(BF16)
    c = (X_HEAD_DIM ** -0.5) * LOG2E
    heads = []
    for hd in range(X_HEADS):
        cols = slice(X_HEAD_DIM * hd, X_HEAD_DIM * (hd + 1))
        s = _dot_nt(q[:, cols], k_ref[:, cols])
        m = jnp.max(s, axis=1, keepdims=True)
        e = jnp.exp2((s - m) * c)
        den = jnp.sum(e, axis=1, keepdims=True)
        heads.append((_dot(e.astype(BF16), v_ref[:, cols]) / den).astype(BF16))
    o = _dot(jnp.concatenate(heads, axis=1), wo_ref[...])
    o_ref[...] = x1 + _rms(o, gpost_ref[...])


def _mix_mem(lidx, x, oa, ob, oc, p, kv, seq):
    t = x.shape[0]
    tm = TM_MEM
    per_b = seq // tm
    return _call(
        _mix_mem_kernel, "mix_mem", (t // tm,),
        in_specs=[_rows(tm, D_MODEL), _rows(tm, MLA_HEADS * MLA_V), _rows(tm, SGU_WIDTH),
                  _rows(tm, DIFF_HEADS * DIFF_V_DIM),
                  _layer(D_MODEL, D_MODEL), _layer(1, D_MODEL), _layer(1, D_MODEL), _layer(D_MODEL, D_MODEL),
                  pl.BlockSpec((None, N_MEM, D_MODEL), lambda i, l: (l[0], i // per_b, 0)),
                  pl.BlockSpec((None, N_MEM, D_MODEL), lambda i, l: (l[0], i // per_b, 1)),
                  _layer(D_MODEL, D_MODEL), _layer(1, D_MODEL)],
        out_specs=_rows(tm, D_MODEL),
        out_shape=jax.ShapeDtypeStruct((t, D_MODEL), F32),
        semantics=("parallel",),
    )(lidx, x, oa, ob, oc, p["w_mix"], p["g_mix_post"], p["g_mem_pre"], p["w_mq"], kv, kv,
      p["w_mo"], p["g_mem_post"])


def _ffn_up_kernel(_, x_ref, xp_ref, xn_ref, g_ref, wg_ref, wu_ref, cwg_ref, cwu_ref, cbg_ref, cbu_ref,
                   o_ref, h_ref, *, seq):
    tm = x_ref.shape[0]
    i = pl.program_id(0)

    @pl.when(pl.program_id(1) == 0)
    def _():
        g = g_ref[...]
        at_start = (i * tm) % seq == 0
        at_end = ((i + 1) * tm) % seq == 0
        h_ref[0:FFN_HALO, :] = jnp.where(at_start, 0.0, _rms(xp_ref[...], g)).astype(BF16)
        h_ref[FFN_HALO:FFN_HALO + tm, :] = _rms(x_ref[...], g).astype(BF16)
        h_ref[FFN_HALO + tm:, :] = jnp.where(at_end, 0.0, _rms(xn_ref[...], g)).astype(BF16)

    hb = h_ref[...]
    rows = tm + 2 * FFN_HALO

    def conv(w_ref, cw_ref, cb_ref):
        a = _dot(hb, w_ref[...])
        cw = cw_ref[...]
        c = cw[0:1] * pltpu.roll(a, 1, 0) + cw[1:2] * a + cw[2:3] * pltpu.roll(a, rows - 1, 0)
        return c[FFN_HALO:FFN_HALO + tm] + cb_ref[...]

    gate = conv(wg_ref, cwg_ref, cbg_ref)
    up = conv(wu_ref, cwu_ref, cbu_ref)
    o_ref[...] = (_gelu(gate) * up).astype(BF16)


def _ffn_up(lidx, x, p, seq):
    t = x.shape[0]
    tm = TM_FFN
    tn = FFN_TN
    nj = D_FF // tn
    hb = tm // FFN_HALO
    last = t // FFN_HALO - 1
    return _call(
        functools.partial(_ffn_up_kernel, seq=seq), "ffn_up", (t // tm, nj),
        in_specs=[pl.BlockSpec((tm, D_MODEL), lambda i, j, l: (i, 0)),
                  pl.BlockSpec((FFN_HALO, D_MODEL), lambda i, j, l: (jnp.maximum(i * hb - 1, 0), 0)),
                  pl.BlockSpec((FFN_HALO, D_MODEL), lambda i, j, l: (jnp.minimum((i + 1) * hb, last), 0)),
                  _layer(1, D_MODEL),
                  pl.BlockSpec((None, D_MODEL, tn), lambda i, j, l: (l[0], 0, j)),
                  pl.BlockSpec((None, D_MODEL, tn), lambda i, j, l: (l[0], 0, j + nj)),
                  pl.BlockSpec((None, CONV_WIDTH, tn), lambda i, j, l: (l[0], 0, j)),
                  pl.BlockSpec((None, CONV_WIDTH, tn), lambda i, j, l: (l[0], 0, j + nj)),
                  pl.BlockSpec((None, 1, tn), lambda i, j, l: (l[0], 0, j)),
                  pl.BlockSpec((None, 1, tn), lambda i, j, l: (l[0], 0, j + nj))],
        out_specs=pl.BlockSpec((tm, tn), lambda i, j, l: (i, j)),
        out_shape=jax.ShapeDtypeStruct((t, D_FF), BF16),
        semantics=("parallel", "arbitrary"),
        scratch=[pltpu.VMEM((tm + 2 * FFN_HALO, D_MODEL), BF16)],
    )(lidx, x, x, x, p["g_ffn_pre"], p["w_up"], p["w_up"], p["conv_w"], p["conv_w"], p["conv_b"], p["conv_b"])


def _ffn_down_kernel(_, x_ref, a_ref, w_ref, g_ref, o_ref):
    o_ref[...] = x_ref[...] + _rms(_dot(a_ref[...], w_ref[...]), g_ref[...])


def _ffn_down(lidx, x, a, p):
    t = x.shape[0]
    tm = TM_FFN
    return _call(
        _ffn_down_kernel, "ffn_down", (t // tm,),
        in_specs=[_rows(tm, D_MODEL), _rows(tm, D_FF), _layer(D_FF, D_MODEL), _layer(1, D_MODEL)],
        out_specs=_rows(tm, D_MODEL),
        out_shape=jax.ShapeDtypeStruct((t, D_MODEL), F32),
        semantics=("parallel",),
    )(lidx, x, a, p["w_down"], p["g_ffn_post"])


def _take_cols(w, idx):
    cols = jnp.take(w, jnp.asarray(np.maximum(idx, 0)), axis=-1)
    return jnp.where(jnp.asarray(idx >= 0), cols, jnp.zeros((), w.dtype))


def _mla_head_lanes():
    src = np.full((HEAD_PAD,), -1, np.int64)
    src[0:16] = MLA_NOPE + np.arange(16)
    src[16:64] = np.arange(48)
    src[64:80] = MLA_NOPE + 16 + np.arange(16)
    src[80:96] = 48 + np.arange(16)
    return src


def _diff_lanes():
    src = np.zeros((256,), np.int64)
    for g in range(DIFF_GROUPS):
        base = g * DIFF_QK_DIM
        for i in range(DIFF_ROT):
            src[g * 8 + i] = base + i
            src[128 + g * 8 + i] = base + DIFF_ROT + i
            src[64 + g * 8 + i] = base + 2 * DIFF_ROT + i
            src[192 + g * 8 + i] = base + 3 * DIFF_ROT + i
    return src


def _layouts():
    head = _mla_head_lanes()
    p_cq, p_ckv, p_kr, p_sgu = MLA_Q_LORA, MLA_KV_LORA, MLA_ROPE, 2 * SGU_WIDTH
    o_kr = p_cq + p_ckv
    o_z = o_kr + p_kr
    o_dq = o_z + p_sgu
    o_dk = o_dq + 256
    o_dv = o_dk + 256
    kr_src = np.where(head >= MLA_NOPE, o_kr + head - MLA_NOPE, -1)
    dl = _diff_lanes()
    win_idx = np.concatenate([np.arange(0, o_kr), kr_src, np.arange(o_z, o_dq), o_dq + dl, o_dk + dl,
                              np.arange(o_dv, o_dv + 256)])
    assert win_idx.shape[0] == IN_PAD
    per_q = MLA_NOPE + MLA_ROPE
    wuq_idx = np.concatenate([np.where(head >= 0, h * per_q + head, -1) for h in range(MLA_HEADS)])
    per_kv = MLA_NOPE + MLA_V
    k_src = np.where((head >= 0) & (head < MLA_NOPE), head, -1)
    wuk_idx = np.concatenate([np.where(k_src >= 0, h * per_kv + k_src, -1) for h in range(MLA_HEADS)])
    wuv_idx = np.concatenate([h * per_kv + MLA_NOPE + np.arange(MLA_V) for h in range(MLA_HEADS)])
    inv_a = ROPE_THETA ** (-jnp.arange(0, MLA_ROPE, 2, dtype=F32) / MLA_ROPE)
    inv_d = ROPE_THETA ** (-jnp.arange(0, 2 * DIFF_ROT, 2, dtype=F32) / (2 * DIFF_ROT))
    fa = jnp.zeros((LANES,), F32).at[0:16].set(inv_a).at[64:80].set(inv_a)
    ga = jnp.zeros((LANES,), F32).at[0:16].set(-1.0).at[64:80].set(1.0)
    fd = jnp.zeros((LANES,), F32).at[0:64].set(jnp.tile(inv_d, 8))
    return win_idx, wuq_idx, wuk_idx, wuv_idx, fa[None], ga[None], fd[None]


def kernel(x, mem, positions, mix_pre_g, mix_post_g, w_in, mla_cq_g, mla_ckv_g, mla_w_uq, mla_w_ukv,
           sgu_norm_g, sgu_w_s, sgu_b_s, diff_lam_q1, diff_lam_k1, diff_lam_q2, diff_lam_k2, diff_sub_g,
           w_mix_out, mem_pre_g, mem_post_g, mem_kv_g, mem_w_q, mem_w_kv, mem_w_o,
           ffn_pre_g, ffn_post_g, ffn_w_up, ffn_conv_w, ffn_conv_b, ffn_w_down):
    batch, seq, d = x.shape
    t = batch * seq
    assert d == D_MODEL and seq % TM_FFN == 0 and seq % TQ == 0 and seq % SGU_CHUNK == 0
    win_idx, wuq_idx, wuk_idx, wuv_idx, fa, ga, fd = _layouts()

    ca, sa, cd, sd = _rope_tables(positions.reshape(t, 1), fa, ga, fd)

    vec = lambda a: a.astype(F32)[:, None, :]
    lam_init = np.array([0.8 - 0.6 * math.exp(-0.3 * l) for l in range(DEPTH)], np.float32)
    w_ukv16 = mla_w_ukv.astype(BF16)
    p = dict(
        g_mix_pre=vec(mix_pre_g), g_mix_post=vec(mix_post_g),
        w_in=_take_cols(w_in.astype(BF16), win_idx),
        g_cq=vec(mla_cq_g), g_ckv=vec(mla_ckv_g),
        w_uq=_take_cols(mla_w_uq.astype(BF16), wuq_idx),
        w_uk=_take_cols(w_ukv16, wuk_idx),
        w_uv=_take_cols(w_ukv16, wuv_idx),
        g_sgu=vec(sgu_norm_g.reshape(DEPTH, SGU_WIDTH)),
        w_s=sgu_w_s.astype(BF16),
        b_s=jnp.repeat(jnp.swapaxes(sgu_b_s.astype(F32), 1, 2), SGU_GROUP_DIM, axis=2),
        lam=jnp.stack([diff_lam_q1, diff_lam_k1, diff_lam_q2, diff_lam_k2], axis=1).astype(F32),
        lam_init=jnp.asarray(lam_init).reshape(DEPTH, 1, 1),
        g_sub=jnp.repeat(diff_sub_g.astype(F32)[:, :, None], LANES, axis=2),
        w_mix=w_mix_out.astype(BF16),
        g_mem_pre=vec(mem_pre_g), g_mem_post=vec(mem_post_g),
        w_mq=mem_w_q.astype(BF16), w_mo=mem_w_o.astype(BF16),
        g_ffn_pre=vec(ffn_pre_g), g_ffn_post=vec(ffn_post_g),
        w_up=ffn_w_up.astype(BF16), conv_w=ffn_conv_w.astype(F32), conv_b=vec(ffn_conv_b),
        w_down=ffn_w_down.astype(BF16),
    )
    kv_all = _mem_kv(mem.reshape(batch * N_MEM, D_MODEL), vec(mem_kv_g), mem_w_kv)

    def layer(xc, lidx):
        qa, ka, va, sgu, qd, kd, vd = _mix_in(lidx, xc, p, ca, sa, cd, sd)
        out_a = _mla_attn(lidx, qa, ka, va, batch, seq)
        out_c = _diff_attn(lidx, p, qd, kd, vd, batch, seq)
        x2 = _mix_mem(lidx, xc, out_a, sgu, out_c, p, kv_all, seq)
        mid = _ffn_up(lidx, x2, p, seq)
        x3 = _ffn_down(lidx, x2, mid, p)
        return x3, None

    out, _ = lax.scan(layer, x.reshape(t, D_MODEL), jnp.arange(DEPTH, dtype=jnp.int32).reshape(DEPTH, 1))
    return out.reshape(batch, seq, D_MODEL)
```

```python
import functools
import math

import jax
import jax.numpy as jnp
import numpy as np
from jax import lax
from jax.experimental import pallas as pl
from jax.experimental.pallas import tpu as pltpu

F32 = jnp.float32
BF16 = jnp.bfloat16

D_MODEL = 1024
DEPTH = 4
N_MEM = 256
ROPE_THETA = 500000.0
EPS = 1e-6

MLA_HEADS = 8
MLA_NOPE = 64
MLA_ROPE = 32
MLA_V = 64
MLA_Q_LORA = 384
MLA_KV_LORA = 256

SGU_GROUPS = 4
SGU_GROUP_DIM = 64
SGU_WIDTH = SGU_GROUPS * SGU_GROUP_DIM
SGU_CHUNK = 128

DIFF_HEADS = 4
DIFF_QK_DIM = 32
DIFF_V_DIM = 64
DIFF_ROT = 8
DIFF_GROUPS = 2 * DIFF_HEADS

X_HEADS = 4
X_HEAD_DIM = D_MODEL // X_HEADS

D_FF = 2816
CONV_WIDTH = 3

LANES = 128
HEAD_PAD = LANES
VT_ROWS = MLA_V + 16
LOG2E = math.log2(math.e)
Q_SCALE_MLA = (MLA_NOPE + MLA_ROPE) ** -0.5 * LOG2E
Q_SCALE_DIFF = DIFF_QK_DIM ** -0.5 * LOG2E
VMEM_LIMIT = 56 * 1024 * 1024

O_CQ = 0
O_CKV = O_CQ + MLA_Q_LORA
O_KR = O_CKV + MLA_KV_LORA
O_Z = O_KR + HEAD_PAD
O_DQ = O_Z + 2 * SGU_WIDTH
O_DK = O_DQ + 256
O_DV = O_DK + 256
IN_PAD = O_DV + 256

TM_IN = 512
IN_PARTS = 4
TQ = 256
KV_CHUNK = 256
ATTN_AHEAD = 6
TM_MEM = 512
TM_FFN = 512
FFN_HALO = 16
FFN_TN = D_FF // 2


def _call(body, name, grid, in_specs, out_specs, out_shape, semantics, scratch=()):
    return pl.pallas_call(
        body,
        grid_spec=pltpu.PrefetchScalarGridSpec(
            num_scalar_prefetch=1, grid=grid, in_specs=in_specs, out_specs=out_specs,
            scratch_shapes=list(scratch)),
        out_shape=out_shape,
        compiler_params=pltpu.CompilerParams(dimension_semantics=semantics, vmem_limit_bytes=VMEM_LIMIT),
        name=name)


def _rows(tm, width):
    return pl.BlockSpec((tm, width), lambda i, l: (i, 0))


def _layer(*tail):
    return pl.BlockSpec((None,) + tail, lambda *a: (a[-1][0],) + (0,) * len(tail))


def _rms(x, g):
    ms = jnp.mean(x * x, axis=-1, keepdims=True)
    return x * lax.rsqrt(ms + EPS) * g


def _gelu(x):
    c = math.sqrt(2.0 / math.pi)
    return 0.5 * x * (1.0 + jnp.tanh(c * (x + 0.044715 * (x * x * x))))


def _dot(a, b):
    return jnp.dot(a, b, preferred_element_type=F32)


def _dot_nt(a, b):
    return lax.dot_general(a, b, (((1,), (1,)), ((), ())), preferred_element_type=F32)


def _rope_kernel(pos_ref, fa_ref, ga_ref, fd_ref, gd_ref, ca_ref, sa_ref, cd_ref, sd_ref):
    pos = pos_ref[...].astype(F32)
    ang_a = pos * fa_ref[...]
    ca_ref[...] = jnp.cos(ang_a)
    sa_ref[...] = jnp.sin(ang_a) * ga_ref[...]
    ang_d = pos * fd_ref[...]
    cd_ref[...] = jnp.cos(ang_d)
    sd_ref[...] = jnp.sin(ang_d) * gd_ref[...]


def _rope_tables(pos, fa, ga, fd, gd):
    t = pos.shape[0]
    tm = 1024
    row = lambda w: pl.BlockSpec((tm, w), lambda i: (i, 0))
    vec = lambda w: pl.BlockSpec((1, w), lambda i: (0, 0))
    return pl.pallas_call(
        _rope_kernel,
        grid=(t // tm,),
        in_specs=[row(1), vec(LANES), vec(LANES), vec(LANES), vec(LANES)],
        out_specs=[row(LANES)] * 4,
        out_shape=[jax.ShapeDtypeStruct((t, LANES), F32)] * 4,
        compiler_params=pltpu.CompilerParams(dimension_semantics=("parallel",), vmem_limit_bytes=VMEM_LIMIT),
        name="rope_tables",
    )(pos, fa, ga, fd, gd)


def _mem_kv_kernel(mem_ref, g_ref, w_ref, o_ref):
    hn = _rms(mem_ref[...], g_ref[...]).astype(BF16)
    o_ref[...] = _dot(hn, w_ref[...].astype(BF16)).astype(BF16)


def _mem_kv(mem2, g, w):
    rows = mem2.shape[0]
    nb = rows // N_MEM
    return pl.pallas_call(
        _mem_kv_kernel,
        grid=(DEPTH, nb),
        in_specs=[pl.BlockSpec((N_MEM, D_MODEL), lambda l, i: (i, 0)),
                  pl.BlockSpec((None, 1, D_MODEL), lambda l, i: (l, 0, 0)),
                  pl.BlockSpec((None, D_MODEL, 2 * D_MODEL), lambda l, i: (l, 0, 0))],
        out_specs=pl.BlockSpec((None, N_MEM, 2 * D_MODEL), lambda l, i: (l, i, 0)),
        out_shape=jax.ShapeDtypeStruct((DEPTH, rows, 2 * D_MODEL), BF16),
        compiler_params=pltpu.CompilerParams(dimension_semantics=("parallel", "parallel"),
                                             vmem_limit_bytes=VMEM_LIMIT),
        name="mem_kv",
    )(mem2, g, w)


def _mix_in_kernel(_, x_ref, g_ref, win_ref, gcq_ref, gckv_ref, wuq_ref, wuk_ref, wuv_ref,
                   ca_ref, sa_ref, cd_ref, sd_ref, gsgu_ref, ws_ref, bs_ref,
                   qa_ref, ka_ref, va_ref, sgu_ref, qd_ref, kd_ref, vd_ref):
    tm = x_ref.shape[0]
    part = tm // IN_PARTS
    parts = [pl.ds(r, part) for r in range(0, tm, part)]
    projs = [_dot(_rms(x_ref[r, :], g_ref[...]).astype(BF16), win_ref[...]) for r in parts]
    for r, proj in zip(parts, projs):
        _mix_in_part(r, proj, gcq_ref, gckv_ref, wuq_ref, wuk_ref, wuv_ref, ca_ref, sa_ref, cd_ref, sd_ref,
                     gsgu_ref, ws_ref, bs_ref, qa_ref, ka_ref, va_ref, sgu_ref, qd_ref, kd_ref, vd_ref)


def _mix_in_part(r, proj, gcq_ref, gckv_ref, wuq_ref, wuk_ref, wuv_ref, ca_ref, sa_ref, cd_ref, sd_ref,
                 gsgu_ref, ws_ref, bs_ref, qa_ref, ka_ref, va_ref, sgu_ref, qd_ref, kd_ref, vd_ref):
    tm = proj.shape[0]
    lane = lax.broadcasted_iota(jnp.int32, (tm, LANES), 1)
    low_half = lane < 64

    ca = ca_ref[r, :]
    sa = sa_ref[r, :]
    cqn = _rms(proj[:, O_CQ:O_CQ + MLA_Q_LORA], gcq_ref[...]).astype(BF16)
    q = _dot(cqn, wuq_ref[...])
    ckvn = _rms(proj[:, O_CKV:O_CKV + MLA_KV_LORA], gckv_ref[...]).astype(BF16)
    kn = _dot(ckvn, wuk_ref[...])
    vv = _dot(ckvn, wuv_ref[...])
    kr = proj[:, O_KR:O_KR + HEAD_PAD]
    kr = kr * ca + pltpu.roll(kr, 64, 1) * sa
    vv_t = vv.T
    ones_row = (lax.broadcasted_iota(jnp.int32, (VT_ROWS - MLA_V, tm), 0) == 0).astype(BF16)
    for hd in range(MLA_HEADS):
        qh = q[:, HEAD_PAD * hd:HEAD_PAD * (hd + 1)]
        qa_ref[hd, r, :] = ((qh * ca + pltpu.roll(qh, 64, 1) * sa) * Q_SCALE_MLA).astype(BF16)
        ka_ref[hd, r, :] = (kn[:, HEAD_PAD * hd:HEAD_PAD * (hd + 1)] + kr).astype(BF16)
        va_ref[hd, 0:MLA_V, r] = vv_t[MLA_V * hd:MLA_V * (hd + 1), :].astype(BF16)
        va_ref[hd, MLA_V:VT_ROWS, r] = ones_row

    zg = _gelu(proj[:, O_Z:O_Z + 2 * SGU_WIDTH])
    u = zg[:, :SGU_WIDTH]
    gsgu = gsgu_ref[...]
    inv_gd = 1.0 / SGU_GROUP_DIM
    vn_tiles = []
    for t in range(SGU_WIDTH // LANES):
        vt = zg[:, SGU_WIDTH + LANES * t:SGU_WIDTH + LANES * (t + 1)]
        s_lo = jnp.sum(jnp.where(low_half, vt, 0.0), axis=1, keepdims=True)
        s_hi = jnp.sum(jnp.where(low_half, 0.0, vt), axis=1, keepdims=True)
        xc = vt - jnp.where(low_half, s_lo, s_hi) * inv_gd
        sq = xc * xc
        v_lo = jnp.sum(jnp.where(low_half, sq, 0.0), axis=1, keepdims=True)
        v_hi = jnp.sum(jnp.where(low_half, 0.0, sq), axis=1, keepdims=True)
        var = jnp.where(low_half, v_lo, v_hi) * inv_gd
        vn_tiles.append((xc * lax.rsqrt(var + EPS) * gsgu[:, LANES * t:LANES * (t + 1)]).astype(BF16))
    bias = bs_ref[...]
    low_chunk = lax.broadcasted_iota(jnp.int32, (SGU_CHUNK, LANES), 1) < 64
    for n in range(tm // SGU_CHUNK):
        rows = slice(SGU_CHUNK * n, SGU_CHUNK * (n + 1))
        out_rows = pl.ds(r.start + SGU_CHUNK * n, SGU_CHUNK)
        for t in range(SGU_WIDTH // LANES):
            vc = vn_tiles[t][rows, :]
            m_lo = _dot(ws_ref[2 * t], vc)
            m_hi = _dot(ws_ref[2 * t + 1], vc)
            mixed = jnp.where(low_chunk, m_lo, m_hi) + bias[:, LANES * t:LANES * (t + 1)]
            sgu_ref[out_rows, LANES * t:LANES * (t + 1)] = (u[rows, LANES * t:LANES * (t + 1)] * mixed).astype(BF16)

    cd = cd_ref[r, :]
    sd = sd_ref[r, :]
    group = (lane & 31) >> 3
    for t in range(2):
        dq = proj[:, O_DQ + LANES * t:O_DQ + LANES * (t + 1)]
        qd_ref[r, LANES * t:LANES * (t + 1)] = (
            (dq * cd + pltpu.roll(dq, 64, 1) * sd) * Q_SCALE_DIFF).astype(BF16)
        dk = proj[:, O_DK + LANES * t:O_DK + LANES * (t + 1)]
        dk = dk * cd + pltpu.roll(dk, 64, 1) * sd
        for gl in range(DIFF_GROUPS // 2):
            kd_ref[4 * t + gl, r, :] = jnp.where(group == gl, dk, 0.0).astype(BF16)
    dv_t = proj[:, O_DV:O_DV + 256].T
    for hd in range(DIFF_HEADS):
        vd_ref[hd, 0:DIFF_V_DIM, r] = dv_t[DIFF_V_DIM * hd:DIFF_V_DIM * (hd + 1), :].astype(BF16)
        vd_ref[hd, DIFF_V_DIM:VT_ROWS, r] = ones_row


def _mix_in(lidx, x, p, ca, sa, cd, sd):
    t = x.shape[0]
    tm = TM_IN
    heads = lambda n, w: pl.BlockSpec((n, tm, w), lambda i, l: (0, i, 0))
    return _call(
        _mix_in_kernel, "mix_in", (t // tm,),
        in_specs=[_rows(tm, D_MODEL), _layer(1, D_MODEL), _layer(D_MODEL, IN_PAD),
                  _layer(1, MLA_Q_LORA), _layer(1, MLA_KV_LORA),
                  _layer(MLA_Q_LORA, MLA_HEADS * HEAD_PAD), _layer(MLA_KV_LORA, MLA_HEADS * HEAD_PAD),
                  _layer(MLA_KV_LORA, MLA_HEADS * MLA_V),
                  _rows(tm, LANES), _rows(tm, LANES), _rows(tm, LANES), _rows(tm, LANES),
                  _layer(1, SGU_WIDTH), _layer(SGU_GROUPS, SGU_CHUNK, SGU_CHUNK), _layer(SGU_CHUNK, SGU_WIDTH)],
        out_specs=[heads(MLA_HEADS, HEAD_PAD), heads(MLA_HEADS, HEAD_PAD),
                   pl.BlockSpec((MLA_HEADS, VT_ROWS, tm), lambda i, l: (0, 0, i)),
                   _rows(tm, SGU_WIDTH), _rows(tm, 256), heads(DIFF_GROUPS, LANES),
                   pl.BlockSpec((DIFF_HEADS, VT_ROWS, tm), lambda i, l: (0, 0, i))],
        out_shape=[jax.ShapeDtypeStruct((MLA_HEADS, t, HEAD_PAD), BF16),
                   jax.ShapeDtypeStruct((MLA_HEADS, t, HEAD_PAD), BF16),
                   jax.ShapeDtypeStruct((MLA_HEADS, VT_ROWS, t), BF16),
                   jax.ShapeDtypeStruct((t, SGU_WIDTH), BF16),
                   jax.ShapeDtypeStruct((t, 256), BF16),
                   jax.ShapeDtypeStruct((DIFF_GROUPS, t, LANES), BF16),
                   jax.ShapeDtypeStruct((DIFF_HEADS, VT_ROWS, t), BF16)],
        semantics=("parallel",),
    )(lidx, x, p["g_mix_pre"], p["w_in"], p["g_cq"], p["g_ckv"], p["w_uq"], p["w_uk"], p["w_uv"],
      ca, sa, cd, sd, p["g_sgu"], p["w_s"], p["b_s"])


def _attend_t(maps, seq, dv):
    n_chunks = seq // KV_CHUNK
    items = [(i, c) for i in range(len(maps)) for c in range(n_chunks)]
    scores = lambda item: _dot_nt(maps[item[0]][0](item[1]), maps[item[0]][1])
    pending = [scores(item) for item in items[:ATTN_AHEAD]]
    outs = []
    m = acc = None
    for n, (i, c) in enumerate(items):
        if n + ATTN_AHEAD < len(items):
            pending.append(scores(items[n + ATTN_AHEAD]))
        s = pending.pop(0)
        mc = jnp.max(s, axis=0, keepdims=True)
        m_new = mc if c == 0 else jnp.maximum(m, mc)
        e = jnp.exp2(s - m_new).astype(BF16)
        r = _dot(maps[i][2](c), e)
        acc = r if c == 0 else acc * jnp.exp2(m - m_new) + r
        m = m_new
        if c == n_chunks - 1:
            outs.append(acc[0:dv] / acc[dv:dv + 1])
    return outs


def _chunk(c):
    return pl.ds(c * KV_CHUNK, KV_CHUNK)


def _mla_attn_kernel(_, q_ref, k_ref, v_ref, o_ref):
    seq = k_ref.shape[1]
    maps = [(lambda c, hd=hd: k_ref[hd, _chunk(c), :], q_ref[hd], lambda c, hd=hd: v_ref[hd, :, _chunk(c)])
            for hd in range(MLA_HEADS)]
    outs = _attend_t(maps, seq, MLA_V)
    for pair in range(MLA_HEADS // 2):
        o_t = jnp.concatenate(outs[2 * pair:2 * pair + 2], axis=0)
        o_ref[:, LANES * pair:LANES * (pair + 1)] = o_t.T.astype(BF16)


def _mla_attn(lidx, qa, ka, va, batch, seq):
    t = qa.shape[1]
    nq = seq // TQ
    return _call(
        _mla_attn_kernel, "mla_attn", (batch, nq),
        in_specs=[pl.BlockSpec((MLA_HEADS, TQ, HEAD_PAD), lambda b, i, l: (0, b * nq + i, 0)),
                  pl.BlockSpec((MLA_HEADS, seq, HEAD_PAD), lambda b, i, l: (0, b, 0)),
                  pl.BlockSpec((MLA_HEADS, VT_ROWS, seq), lambda b, i, l: (0, 0, b))],
        out_specs=pl.BlockSpec((TQ, MLA_HEADS * MLA_V), lambda b, i, l: (b * nq + i, 0)),
        out_shape=jax.ShapeDtypeStruct((t, MLA_HEADS * MLA_V), BF16),
        semantics=("parallel", "arbitrary"),
    )(lidx, qa, ka, va)


def _diff_attn_kernel(_, lam_ref, linit_ref, gsub_ref, q_ref, k_ref, v_ref, o_ref):
    lam_init = linit_ref[...]
    lam = (jnp.exp(jnp.sum(lam_ref[0:1, :] * lam_ref[1:2, :], axis=1, keepdims=True))
           - jnp.exp(jnp.sum(lam_ref[2:3, :] * lam_ref[3:4, :], axis=1, keepdims=True)) + lam_init)
    gsub = gsub_ref[...] * (1.0 - lam_init)
    gsub = jnp.concatenate([gsub] * (TQ // LANES), axis=1)
    seq = k_ref.shape[1]
    q = q_ref[...]
    maps = [(lambda c, g=g: k_ref[g, _chunk(c), :], q[:, LANES * (g // 4):LANES * (g // 4 + 1)],
             lambda c, g=g: v_ref[g // 2, :, _chunk(c)]) for g in range(DIFF_GROUPS)]
    outs = _attend_t(maps, seq, DIFF_V_DIM)
    normed = []
    for hd in range(DIFF_HEADS):
        o = outs[2 * hd] - lam * outs[2 * hd + 1]
        ms = jnp.mean(o * o, axis=0, keepdims=True)
        normed.append(o * lax.rsqrt(ms + EPS) * gsub)
    for pair in range(DIFF_HEADS // 2):
        o_t = jnp.concatenate(normed[2 * pair:2 * pair + 2], axis=0)
        o_ref[:, LANES * pair:LANES * (pair + 1)] = o_t.T.astype(BF16)


def _diff_attn(lidx, p, qd, kd, vd, batch, seq):
    t = qd.shape[0]
    nq = seq // TQ
    return _call(
        _diff_attn_kernel, "diff_attn", (batch, nq),
        in_specs=[_layer(4, DIFF_QK_DIM), _layer(1, 1), _layer(DIFF_V_DIM, LANES),
                  pl.BlockSpec((TQ, 256), lambda b, i, l: (b * nq + i, 0)),
                  pl.BlockSpec((DIFF_GROUPS, seq, LANES), lambda b, i, l: (0, b, 0)),
                  pl.BlockSpec((DIFF_HEADS, VT_ROWS, seq), lambda b, i, l: (0, 0, b))],
        out_specs=pl.BlockSpec((TQ, 256), lambda b, i, l: (b * nq + i, 0)),
        out_shape=jax.ShapeDtypeStruct((t, DIFF_HEADS * DIFF_V_DIM), BF16),
        semantics=("parallel", "arbitrary"),
    )(lidx, p["lam"], p["lam_init"], p["g_sub"], qd, kd, vd)


def _mix_mem_kernel(_, x_ref, oa_ref, ob_ref, oc_ref, wmix_ref, gmix_ref, gpre_ref, wq_ref,
                    k_ref, v_ref, wo_ref, gpost_ref, o_ref):
    cat = jnp.concatenate([oa_ref[...], ob_ref[...], oc_ref[...]], axis=1)
    x1 = x_ref[...] + _rms(_dot(cat, wmix_ref[...]), gmix_ref[...])
    hq = _rms(x1, gpre_ref[...]).astype(BF16)
    q = _dot(hq, wq_ref[...]).astype(BF16)
    c = (X_HEAD_DIM ** -0.5) * LOG2E
    heads = []
    for hd in range(X_HEADS):
        cols = slice(X_HEAD_DIM * hd, X_HEAD_DIM * (hd + 1))
        s = _dot_nt(q[:, cols], k_ref[:, cols])
        m = jnp.max(s, axis=1, keepdims=True)
        e = jnp.exp2((s - m) * c)
        den = jnp.sum(e, axis=1, keepdims=True)
        heads.append((_dot(e.astype(BF16), v_ref[:, cols]) / den).astype(BF16))
    o = _dot(jnp.concatenate(heads, axis=1), wo_ref[...])
    o_ref[...] = x1 + _rms(o, gpost_ref[...])


def _mix_mem(lidx, x, oa, ob, oc, p, kv, seq):
    t = x.shape[0]
    tm = TM_MEM
    per_b = seq // tm
    return _call(
        _mix_mem_kernel, "mix_mem", (t // tm,),
        in_specs=[_rows(tm, D_MODEL), _rows(tm, MLA_HEADS * MLA_V), _rows(tm, SGU_WIDTH),
                  _rows(tm, DIFF_HEADS * DIFF_V_DIM),
                  _layer(D_MODEL, D_MODEL), _layer(1, D_MODEL), _layer(1, D_MODEL), _layer(D_MODEL, D_MODEL),
                  pl.BlockSpec((None, N_MEM, D_MODEL), lambda i, l: (l[0], i // per_b, 0)),
                  pl.BlockSpec((None, N_MEM, D_MODEL), lambda i, l: (l[0], i // per_b, 1)),
                  _layer(D_MODEL, D_MODEL), _layer(1, D_MODEL)],
        out_specs=_rows(tm, D_MODEL),
        out_shape=jax.ShapeDtypeStruct((t, D_MODEL), F32),
        semantics=("parallel",),
    )(lidx, x, oa, ob, oc, p["w_mix"], p["g_mix_post"], p["g_mem_pre"], p["w_mq"], kv, kv,
      p["w_mo"], p["g_mem_post"])


def _ffn_up_kernel(_, x_ref, xp_ref, xn_ref, g_ref, wg_ref, wu_ref, cwg_ref, cwu_ref, cbg_ref, cbu_ref,
                   o_ref, h_ref, *, seq):
    tm = x_ref.shape[0]
    i = pl.program_id(0)

    @pl.when(pl.program_id(1) == 0)
    def _():
        g = g_ref[...]
        at_start = (i * tm) % seq == 0
        at_end = ((i + 1) * tm) % seq == 0
        h_ref[0:FFN_HALO, :] = jnp.where(at_start, 0.0, _rms(xp_ref[...], g)).astype(BF16)
        h_ref[FFN_HALO:FFN_HALO + tm, :] = _rms(x_ref[...], g).astype(BF16)
        h_ref[FFN_HALO + tm:, :] = jnp.where(at_end, 0.0, _rms(xn_ref[...], g)).astype(BF16)

    hb = h_ref[...]
    rows = tm + 2 * FFN_HALO

    def conv(w_ref, cw_ref, cb_ref):
        a = _dot(hb, w_ref[...])
        cw = cw_ref[...]
        c = cw[0:1] * pltpu.roll(a, 1, 0) + cw[1:2] * a + cw[2:3] * pltpu.roll(a, rows - 1, 0)
        return c[FFN_HALO:FFN_HALO + tm] + cb_ref[...]

    gate = conv(wg_ref, cwg_ref, cbg_ref)
    up = conv(wu_ref, cwu_ref, cbu_ref)
    o_ref[...] = (_gelu(gate) * up).astype(BF16)


def _ffn_up(lidx, x, p, seq):
    t = x.shape[0]
    tm = TM_FFN
    tn = FFN_TN
    nj = D_FF // tn
    hb = tm // FFN_HALO
    last = t // FFN_HALO - 1
    return _call(
        functools.partial(_ffn_up_kernel, seq=seq), "ffn_up", (t // tm, nj),
        in_specs=[pl.BlockSpec((tm, D_MODEL), lambda i, j, l: (i, 0)),
                  pl.BlockSpec((FFN_HALO, D_MODEL), lambda i, j, l: (jnp.maximum(i * hb - 1, 0), 0)),
                  pl.BlockSpec((FFN_HALO, D_MODEL), lambda i, j, l: (jnp.minimum((i + 1) * hb, last), 0)),
                  _layer(1, D_MODEL),
                  pl.BlockSpec((None, D_MODEL, tn), lambda i, j, l: (l[0], 0, j)),
                  pl.BlockSpec((None, D_MODEL, tn), lambda i, j, l: (l[0], 0, j + nj)),
                  pl.BlockSpec((None, CONV_WIDTH, tn), lambda i, j, l: (l[0], 0, j)),
                  pl.BlockSpec((None, CONV_WIDTH, tn), lambda i, j, l: (l[0], 0, j + nj)),
                  pl.BlockSpec((None, 1, tn), lambda i, j, l: (l[0], 0, j)),
                  pl.BlockSpec((None, 1, tn), lambda i, j, l: (l[0], 0, j + nj))],
        out_specs=pl.BlockSpec((tm, tn), lambda i, j, l: (i, j)),
        out_shape=jax.ShapeDtypeStruct((t, D_FF), BF16),
        semantics=("parallel", "arbitrary"),
        scratch=[pltpu.VMEM((tm + 2 * FFN_HALO, D_MODEL), BF16)],
    )(lidx, x, x, x, p["g_ffn_pre"], p["w_up"], p["w_up"], p["conv_w"], p["conv_w"], p["conv_b"], p["conv_b"])


def _ffn_down_kernel(_, x_ref, a_ref, w_ref, g_ref, o_ref):
    o_ref[...] = x_ref[...] + _rms(_dot(a_ref[...], w_ref[...]), g_ref[...])


def _ffn_down(lidx, x, a, p):
    t = x.shape[0]
    tm = TM_FFN
    return _call(
        _ffn_down_kernel, "ffn_down", (t // tm,),
        in_specs=[_rows(tm, D_MODEL), _rows(tm, D_FF), _layer(D_FF, D_MODEL), _layer(1, D_MODEL)],
        out_specs=_rows(tm, D_MODEL),
        out_shape=jax.ShapeDtypeStruct((t, D_MODEL), F32),
        semantics=("parallel",),
    )(lidx, x, a, p["w_down"], p["g_ffn_post"])


def _take_cols(w, idx):
    cols = jnp.take(w, jnp.asarray(np.maximum(idx, 0)), axis=-1)
    return jnp.where(jnp.asarray(idx >= 0), cols, jnp.zeros((), w.dtype))


def _mla_head_lanes():
    src = np.full((HEAD_PAD,), -1, np.int64)
    src[0:16] = MLA_NOPE + np.arange(16)
    src[16:64] = np.arange(48)
    src[64:80] = MLA_NOPE + 16 + np.arange(16)
    src[80:96] = 48 + np.arange(16)
    return src


def _diff_lanes():
    src = np.zeros((256,), np.int64)
    for g in range(DIFF_GROUPS):
        base = g * DIFF_QK_DIM
        lane0 = LANES * (g // 4) + DIFF_ROT * (g % 4)
        for i in range(DIFF_ROT):
            src[lane0 + i] = base + i
            src[lane0 + 64 + i] = base + DIFF_ROT + i
            src[lane0 + 32 + i] = base + 2 * DIFF_ROT + i
            src[lane0 + 96 + i] = base + 3 * DIFF_ROT + i
    return src


def _layouts():
    head = _mla_head_lanes()
    p_cq, p_ckv, p_kr, p_sgu = MLA_Q_LORA, MLA_KV_LORA, MLA_ROPE, 2 * SGU_WIDTH
    o_kr = p_cq + p_ckv
    o_z = o_kr + p_kr
    o_dq = o_z + p_sgu
    o_dk = o_dq + 256
    o_dv = o_dk + 256
    kr_src = np.where(head >= MLA_NOPE, head - MLA_NOPE, -1)
    dl = _diff_lanes()

    def win_layout(w):
        out = jnp.concatenate([w[..., :o_kr], _take_cols(w[..., o_kr:o_z], kr_src), w[..., o_z:o_dq],
                               _take_cols(w[..., o_dq:o_dk], dl), _take_cols(w[..., o_dk:o_dv], dl),
                               w[..., o_dv:]], axis=-1)
        assert out.shape[-1] == IN_PAD
        return out
    per_q = MLA_NOPE + MLA_ROPE
    wuq_idx = np.concatenate([np.where(head >= 0, h * per_q + head, -1) for h in range(MLA_HEADS)])
    per_kv = MLA_NOPE + MLA_V
    k_src = np.where((head >= 0) & (head < MLA_NOPE), head, -1)
    wuk_idx = np.concatenate([np.where(k_src >= 0, h * per_kv + k_src, -1) for h in range(MLA_HEADS)])
    wuv_idx = np.concatenate([h * per_kv + MLA_NOPE + np.arange(MLA_V) for h in range(MLA_HEADS)])
    inv_a = ROPE_THETA ** (-jnp.arange(0, MLA_ROPE, 2, dtype=F32) / MLA_ROPE)
    inv_d = ROPE_THETA ** (-jnp.arange(0, 2 * DIFF_ROT, 2, dtype=F32) / (2 * DIFF_ROT))
    fa = jnp.zeros((LANES,), F32).at[0:16].set(inv_a).at[64:80].set(inv_a)
    ga = jnp.zeros((LANES,), F32).at[0:16].set(-1.0).at[64:80].set(1.0)
    fd = jnp.zeros((LANES,), F32).at[0:32].set(jnp.tile(inv_d, 4)).at[64:96].set(jnp.tile(inv_d, 4))
    gd = jnp.zeros((LANES,), F32).at[0:32].set(-1.0).at[64:96].set(1.0)
    return win_layout, wuq_idx, wuk_idx, wuv_idx, fa[None], ga[None], fd[None], gd[None]


def kernel(x, mem, positions, mix_pre_g, mix_post_g, w_in, mla_cq_g, mla_ckv_g, mla_w_uq, mla_w_ukv,
           sgu_norm_g, sgu_w_s, sgu_b_s, diff_lam_q1, diff_lam_k1, diff_lam_q2, diff_lam_k2, diff_sub_g,
           w_mix_out, mem_pre_g, mem_post_g, mem_kv_g, mem_w_q, mem_w_kv, mem_w_o,
           ffn_pre_g, ffn_post_g, ffn_w_up, ffn_conv_w, ffn_conv_b, ffn_w_down):
    batch, seq, d = x.shape
    t = batch * seq
    assert d == D_MODEL and seq % TM_FFN == 0 and seq % TQ == 0 and seq % SGU_CHUNK == 0
    win_layout, wuq_idx, wuk_idx, wuv_idx, fa, ga, fd, gd = _layouts()

    ca, sa, cd, sd = _rope_tables(positions.reshape(t, 1), fa, ga, fd, gd)

    vec = lambda a: a.astype(F32)[:, None, :]
    lam_init = np.array([0.8 - 0.6 * math.exp(-0.3 * l) for l in range(DEPTH)], np.float32)
    w_ukv16 = mla_w_ukv.astype(BF16)
    p = dict(
        g_mix_pre=vec(mix_pre_g), g_mix_post=vec(mix_post_g),
        w_in=win_layout(w_in.astype(BF16)),
        g_cq=vec(mla_cq_g), g_ckv=vec(mla_ckv_g),
        w_uq=_take_cols(mla_w_uq.astype(BF16), wuq_idx),
        w_uk=_take_cols(w_ukv16, wuk_idx),
        w_uv=_take_cols(w_ukv16, wuv_idx),
        g_sgu=vec(sgu_norm_g.reshape(DEPTH, SGU_WIDTH)),
        w_s=sgu_w_s.astype(BF16),
        b_s=jnp.repeat(jnp.swapaxes(sgu_b_s.astype(F32), 1, 2), SGU_GROUP_DIM, axis=2),
        lam=jnp.stack([diff_lam_q1, diff_lam_k1, diff_lam_q2, diff_lam_k2], axis=1).astype(F32),
        lam_init=jnp.asarray(lam_init).reshape(DEPTH, 1, 1),
        g_sub=jnp.repeat(diff_sub_g.astype(F32)[:, :, None], LANES, axis=2),
        w_mix=w_mix_out.astype(BF16),
        g_mem_pre=vec(mem_pre_g), g_mem_post=vec(mem_post_g),
        w_mq=mem_w_q.astype(BF16), w_mo=mem_w_o.astype(BF16),
        g_ffn_pre=vec(ffn_pre_g), g_ffn_post=vec(ffn_post_g),
        w_up=ffn_w_up.astype(BF16), conv_w=ffn_conv_w.astype(F32), conv_b=vec(ffn_conv_b),
        w_down=ffn_w_down.astype(BF16),
    )
    kv_all = _mem_kv(mem.reshape(batch * N_MEM, D_MODEL), vec(mem_kv_g), mem_w_kv)

    xc = x.reshape(t, D_MODEL)
    for l in range(DEPTH):
        lidx = jnp.full((1,), l, jnp.int32)
        qa, ka, va, sgu, qd, kd, vd = _mix_in(lidx, xc, p, ca, sa, cd, sd)
        out_a = _mla_attn(lidx, qa, ka, va, batch, seq)
        out_c = _diff_attn(lidx, p, qd, kd, vd, batch, seq)
        x2 = _mix_mem(lidx, xc, out_a, sgu, out_c, p, kv_all, seq)
        mid = _ffn_up(lidx, x2, p, seq)
        xc = _ffn_down(lidx, x2, mid, p)
    return xc.reshape(batch, seq, D_MODEL)
```

```python
import functools
import math

import jax
import jax.numpy as jnp
import numpy as np
from jax import lax
from jax.experimental import pallas as pl
from jax.experimental.pallas import tpu as pltpu

F32 = jnp.float32
BF16 = jnp.bfloat16

D_MODEL = 1024
DEPTH = 4
N_MEM = 256
ROPE_THETA = 500000.0
EPS = 1e-6

MLA_HEADS = 8
MLA_NOPE = 64
MLA_ROPE = 32
MLA_V = 64
MLA_Q_LORA = 384
MLA_KV_LORA = 256

SGU_GROUPS = 4
SGU_GROUP_DIM = 64
SGU_WIDTH = SGU_GROUPS * SGU_GROUP_DIM
SGU_CHUNK = 128

DIFF_HEADS = 4
DIFF_QK_DIM = 32
DIFF_V_DIM = 64
DIFF_ROT = 8
DIFF_GROUPS = 2 * DIFF_HEADS

X_HEADS = 4
X_HEAD_DIM = D_MODEL // X_HEADS

D_FF = 2816
CONV_WIDTH = 3

LANES = 128
HEAD_PAD = LANES
VT_ROWS = MLA_V + 16
LOG2E = math.log2(math.e)
Q_SCALE_MLA = (MLA_NOPE + MLA_ROPE) ** -0.5 * LOG2E
Q_SCALE_DIFF = DIFF_QK_DIM ** -0.5 * LOG2E
VMEM_LIMIT = 56 * 1024 * 1024

O_CQ = 0
O_CKV = O_CQ + MLA_Q_LORA
O_KR = O_CKV + MLA_KV_LORA
O_Z = O_KR + HEAD_PAD
O_DQ = O_Z + 2 * SGU_WIDTH
O_DK = O_DQ + 256
O_DV = O_DK + 256
IN_PAD = O_DV + 256

TM_IN = 512
IN_PARTS = 4
TQ = 512
KV_CHUNK = 256
ATTN_AHEAD = 6
TM_MEM = 512
TM_KV = 1024
TM_FFN = 512
FFN_HALO = 16
FFN_TN = D_FF // 2


def _call(body, name, grid, in_specs, out_specs, out_shape, semantics, scratch=()):
    return pl.pallas_call(
        body,
        grid_spec=pltpu.PrefetchScalarGridSpec(
            num_scalar_prefetch=1, grid=grid, in_specs=in_specs, out_specs=out_specs,
            scratch_shapes=list(scratch)),
        out_shape=out_shape,
        compiler_params=pltpu.CompilerParams(dimension_semantics=semantics, vmem_limit_bytes=VMEM_LIMIT),
        name=name)


def _rows(tm, width):
    return pl.BlockSpec((tm, width), lambda i, l: (i, 0))


def _layer(*tail):
    return pl.BlockSpec((None,) + tail, lambda *a: (a[-1][0],) + (0,) * len(tail))


def _rms(x, g):
    ms = jnp.mean(x * x, axis=-1, keepdims=True)
    return x * lax.rsqrt(ms + EPS) * g


def _gelu(x):
    c = math.sqrt(2.0 / math.pi)
    return 0.5 * x * (1.0 + jnp.tanh(c * (x + 0.044715 * (x * x * x))))


def _dot(a, b):
    return jnp.dot(a, b, preferred_element_type=F32)


def _dot_nt(a, b):
    return lax.dot_general(a, b, (((1,), (1,)), ((), ())), preferred_element_type=F32)


def _rope_kernel(pos_ref, fa_ref, ga_ref, fd_ref, gd_ref, ca_ref, sa_ref, cd_ref, sd_ref):
    pos = pos_ref[...].astype(F32)
    ang_a = pos * fa_ref[...]
    ca_ref[...] = jnp.cos(ang_a)
    sa_ref[...] = jnp.sin(ang_a) * ga_ref[...]
    ang_d = pos * fd_ref[...]
    cd_ref[...] = jnp.cos(ang_d)
    sd_ref[...] = jnp.sin(ang_d) * gd_ref[...]


def _rope_tables(pos, fa, ga, fd, gd):
    t = pos.shape[0]
    tm = 1024
    row = lambda w: pl.BlockSpec((tm, w), lambda i: (i, 0))
    vec = lambda w: pl.BlockSpec((1, w), lambda i: (0, 0))
    return pl.pallas_call(
        _rope_kernel,
        grid=(t // tm,),
        in_specs=[row(1), vec(LANES), vec(LANES), vec(LANES), vec(LANES)],
        out_specs=[row(LANES)] * 4,
        out_shape=[jax.ShapeDtypeStruct((t, LANES), F32)] * 4,
        compiler_params=pltpu.CompilerParams(dimension_semantics=("parallel",), vmem_limit_bytes=VMEM_LIMIT),
        name="rope_tables",
    )(pos, fa, ga, fd, gd)


def _mem_kv_kernel(mem_ref, g_ref, w_ref, o_ref, w16_ref):
    @pl.when(pl.program_id(1) == 0)
    def _():
        w16_ref[...] = w_ref[...].astype(BF16)

    hn = _rms(mem_ref[...], g_ref[...]).astype(BF16)
    o_ref[...] = _dot(hn, w16_ref[...]).astype(BF16)


def _mem_kv(mem2, g, w):
    rows = mem2.shape[0]
    tm = TM_KV
    return pl.pallas_call(
        _mem_kv_kernel,
        grid=(DEPTH, rows // tm),
        in_specs=[pl.BlockSpec((tm, D_MODEL), lambda l, i: (i, 0)),
                  pl.BlockSpec((None, 1, D_MODEL), lambda l, i: (l, 0, 0)),
                  pl.BlockSpec((None, D_MODEL, 2 * D_MODEL), lambda l, i: (l, 0, 0))],
        out_specs=pl.BlockSpec((None, tm, 2 * D_MODEL), lambda l, i: (l, i, 0)),
        out_shape=jax.ShapeDtypeStruct((DEPTH, rows, 2 * D_MODEL), BF16),
        scratch_shapes=[pltpu.VMEM((D_MODEL, 2 * D_MODEL), BF16)],
        compiler_params=pltpu.CompilerParams(dimension_semantics=("arbitrary", "arbitrary"),
                                             vmem_limit_bytes=VMEM_LIMIT),
        name="mem_kv",
    )(mem2, g, w)


def _mix_in_kernel(_, x_ref, g_ref, win_ref, gcq_ref, gckv_ref, wuq_ref, wuk_ref, wuv_ref,
                   ca_ref, sa_ref, cd_ref, sd_ref, gsgu_ref, ws_ref, bs_ref,
                   qa_ref, ka_ref, va_ref, sgu_ref, qd_ref, kd_ref, vd_ref):
    tm = x_ref.shape[0]
    part = tm // IN_PARTS
    parts = [pl.ds(r, part) for r in range(0, tm, part)]
    projs = [_dot(_rms(x_ref[r, :], g_ref[...]).astype(BF16), win_ref[...]) for r in parts]
    for r, proj in zip(parts, projs):
        _mix_in_part(r, proj, gcq_ref, gckv_ref, wuq_ref, wuk_ref, wuv_ref, ca_ref, sa_ref, cd_ref, sd_ref,
                     gsgu_ref, ws_ref, bs_ref, qa_ref, ka_ref, va_ref, sgu_ref, qd_ref, kd_ref, vd_ref)


def _mix_in_part(r, proj, gcq_ref, gckv_ref, wuq_ref, wuk_ref, wuv_ref, ca_ref, sa_ref, cd_ref, sd_ref,
                 gsgu_ref, ws_ref, bs_ref, qa_ref, ka_ref, va_ref, sgu_ref, qd_ref, kd_ref, vd_ref):
    tm = proj.shape[0]
    lane = lax.broadcasted_iota(jnp.int32, (tm, LANES), 1)
    low_half = lane < 64

    ca = ca_ref[r, :]
    sa = sa_ref[r, :]
    cqn = _rms(proj[:, O_CQ:O_CQ + MLA_Q_LORA], gcq_ref[...]).astype(BF16)
    q = _dot(cqn, wuq_ref[...])
    ckvn = _rms(proj[:, O_CKV:O_CKV + MLA_KV_LORA], gckv_ref[...]).astype(BF16)
    kn = _dot(ckvn, wuk_ref[...])
    vv = _dot(ckvn, wuv_ref[...])
    kr = proj[:, O_KR:O_KR + HEAD_PAD]
    kr = kr * ca + pltpu.roll(kr, 64, 1) * sa
    vv_t = vv.T
    ones_row = (lax.broadcasted_iota(jnp.int32, (VT_ROWS - MLA_V, tm), 0) == 0).astype(BF16)
    for hd in range(MLA_HEADS):
        qh = q[:, HEAD_PAD * hd:HEAD_PAD * (hd + 1)]
        qa_ref[hd, r, :] = ((qh * ca + pltpu.roll(qh, 64, 1) * sa) * Q_SCALE_MLA).astype(BF16)
        ka_ref[hd, r, :] = (kn[:, HEAD_PAD * hd:HEAD_PAD * (hd + 1)] + kr).astype(BF16)
        va_ref[hd, 0:MLA_V, r] = vv_t[MLA_V * hd:MLA_V * (hd + 1), :].astype(BF16)
        va_ref[hd, MLA_V:VT_ROWS, r] = ones_row

    zg = _gelu(proj[:, O_Z:O_Z + 2 * SGU_WIDTH])
    u = zg[:, :SGU_WIDTH]
    gsgu = gsgu_ref[...]
    inv_gd = 1.0 / SGU_GROUP_DIM
    vn_tiles = []
    for t in range(SGU_WIDTH // LANES):
        vt = zg[:, SGU_WIDTH + LANES * t:SGU_WIDTH + LANES * (t + 1)]
        s_lo = jnp.sum(jnp.where(low_half, vt, 0.0), axis=1, keepdims=True)
        s_hi = jnp.sum(jnp.where(low_half, 0.0, vt), axis=1, keepdims=True)
        xc = vt - jnp.where(low_half, s_lo, s_hi) * inv_gd
        sq = xc * xc
        v_lo = jnp.sum(jnp.where(low_half, sq, 0.0), axis=1, keepdims=True)
        v_hi = jnp.sum(jnp.where(low_half, 0.0, sq), axis=1, keepdims=True)
        var = jnp.where(low_half, v_lo, v_hi) * inv_gd
        vn_tiles.append((xc * lax.rsqrt(var + EPS) * gsgu[:, LANES * t:LANES * (t + 1)]).astype(BF16))
    bias = bs_ref[...]
    low_chunk = lax.broadcasted_iota(jnp.int32, (SGU_CHUNK, LANES), 1) < 64
    for n in range(tm // SGU_CHUNK):
        rows = slice(SGU_CHUNK * n, SGU_CHUNK * (n + 1))
        out_rows = pl.ds(r.start + SGU_CHUNK * n, SGU_CHUNK)
        for t in range(SGU_WIDTH // LANES):
            vc = vn_tiles[t][rows, :]
            m_lo = _dot(ws_ref[2 * t], vc)
            m_hi = _dot(ws_ref[2 * t + 1], vc)
            mixed = jnp.where(low_chunk, m_lo, m_hi) + bias[:, LANES * t:LANES * (t + 1)]
            sgu_ref[out_rows, LANES * t:LANES * (t + 1)] = (u[rows, LANES * t:LANES * (t + 1)] * mixed).astype(BF16)

    cd = cd_ref[r, :]
    sd = sd_ref[r, :]
    group = (lane & 31) >> 3
    for t in range(2):
        dq = proj[:, O_DQ + LANES * t:O_DQ + LANES * (t + 1)]
        qd_ref[r, LANES * t:LANES * (t + 1)] = (
            (dq * cd + pltpu.roll(dq, 64, 1) * sd) * Q_SCALE_DIFF).astype(BF16)
        dk = proj[:, O_DK + LANES * t:O_DK + LANES * (t + 1)]
        dk = dk * cd + pltpu.roll(dk, 64, 1) * sd
        for gl in range(DIFF_GROUPS // 2):
            kd_ref[4 * t + gl, r, :] = jnp.where(group == gl, dk, 0.0).astype(BF16)
    dv_t = proj[:, O_DV:O_DV + 256].T
    for hd in range(DIFF_HEADS):
        vd_ref[hd, 0:DIFF_V_DIM, r] = dv_t[DIFF_V_DIM * hd:DIFF_V_DIM * (hd + 1), :].astype(BF16)
        vd_ref[hd, DIFF_V_DIM:VT_ROWS, r] = ones_row


def _mix_in(lidx, x, p, ca, sa, cd, sd):
    t = x.shape[0]
    tm = TM_IN
    heads = lambda n, w: pl.BlockSpec((n, tm, w), lambda i, l: (0, i, 0))
    return _call(
        _mix_in_kernel, "mix_in", (t // tm,),
        in_specs=[_rows(tm, D_MODEL), _layer(1, D_MODEL), _layer(D_MODEL, IN_PAD),
                  _layer(1, MLA_Q_LORA), _layer(1, MLA_KV_LORA),
                  _layer(MLA_Q_LORA, MLA_HEADS * HEAD_PAD), _layer(MLA_KV_LORA, MLA_HEADS * HEAD_PAD),
                  _layer(MLA_KV_LORA, MLA_HEADS * MLA_V),
                  _rows(tm, LANES), _rows(tm, LANES), _rows(tm, LANES), _rows(tm, LANES),
                  _layer(1, SGU_WIDTH), _layer(SGU_GROUPS, SGU_CHUNK, SGU_CHUNK), _layer(SGU_CHUNK, SGU_WIDTH)],
        out_specs=[heads(MLA_HEADS, HEAD_PAD), heads(MLA_HEADS, HEAD_PAD),
                   pl.BlockSpec((MLA_HEADS, VT_ROWS, tm), lambda i, l: (0, 0, i)),
                   _rows(tm, SGU_WIDTH), _rows(tm, 256), heads(DIFF_GROUPS, LANES),
                   pl.BlockSpec((DIFF_HEADS, VT_ROWS, tm), lambda i, l: (0, 0, i))],
        out_shape=[jax.ShapeDtypeStruct((MLA_HEADS, t, HEAD_PAD), BF16),
                   jax.ShapeDtypeStruct((MLA_HEADS, t, HEAD_PAD), BF16),
                   jax.ShapeDtypeStruct((MLA_HEADS, VT_ROWS, t), BF16),
                   jax.ShapeDtypeStruct((t, SGU_WIDTH), BF16),
                   jax.ShapeDtypeStruct((t, 256), BF16),
                   jax.ShapeDtypeStruct((DIFF_GROUPS, t, LANES), BF16),
                   jax.ShapeDtypeStruct((DIFF_HEADS, VT_ROWS, t), BF16)],
        semantics=("parallel",),
    )(lidx, x, p["g_mix_pre"], p["w_in"], p["g_cq"], p["g_ckv"], p["w_uq"], p["w_uk"], p["w_uv"],
      ca, sa, cd, sd, p["g_sgu"], p["w_s"], p["b_s"])


def _attend_t(maps, seq, dv):
    n_chunks = seq // KV_CHUNK
    items = [(i, c) for i in range(len(maps)) for c in range(n_chunks)]
    scores = lambda item: _dot_nt(maps[item[0]][0](item[1]), maps[item[0]][1])
    pending = [scores(item) for item in items[:ATTN_AHEAD]]
    outs = []
    m = acc = None
    for n, (i, c) in enumerate(items):
        if n + ATTN_AHEAD < len(items):
            pending.append(scores(items[n + ATTN_AHEAD]))
        s = pending.pop(0)
        mc = jnp.max(s, axis=0, keepdims=True)
        m_new = mc if c == 0 else jnp.maximum(m, mc)
        e = jnp.exp2(s - m_new).astype(BF16)
        r = _dot(maps[i][2](c), e)
        acc = r if c == 0 else acc * jnp.exp2(m - m_new) + r
        m = m_new
        if c == n_chunks - 1:
            outs.append(acc[0:dv] / acc[dv:dv + 1])
    return outs


def _chunk(c):
    return pl.ds(c * KV_CHUNK, KV_CHUNK)


def _mla_attn_kernel(_, q_ref, k_ref, v_ref, o_ref):
    seq = k_ref.shape[1]
    maps = [(lambda c, hd=hd: k_ref[hd, _chunk(c), :], q_ref[hd], lambda c, hd=hd: v_ref[hd, :, _chunk(c)])
            for hd in range(MLA_HEADS)]
    outs = _attend_t(maps, seq, MLA_V)
    for pair in range(MLA_HEADS // 2):
        o_t = jnp.concatenate(outs[2 * pair:2 * pair + 2], axis=0)
        o_ref[:, LANES * pair:LANES * (pair + 1)] = o_t.T.astype(BF16)


def _mla_attn(lidx, qa, ka, va, batch, seq):
    t = qa.shape[1]
    nq = seq // TQ
    return _call(
        _mla_attn_kernel, "mla_attn", (batch, nq),
        in_specs=[pl.BlockSpec((MLA_HEADS, TQ, HEAD_PAD), lambda b, i, l: (0, b * nq + i, 0)),
                  pl.BlockSpec((MLA_HEADS, seq, HEAD_PAD), lambda b, i, l: (0, b, 0)),
                  pl.BlockSpec((MLA_HEADS, VT_ROWS, seq), lambda b, i, l: (0, 0, b))],
        out_specs=pl.BlockSpec((TQ, MLA_HEADS * MLA_V), lambda b, i, l: (b * nq + i, 0)),
        out_shape=jax.ShapeDtypeStruct((t, MLA_HEADS * MLA_V), BF16),
        semantics=("parallel", "arbitrary"),
    )(lidx, qa, ka, va)


def _diff_attn_kernel(_, lam_ref, linit_ref, gsub_ref, q_ref, k_ref, v_ref, o_ref):
    lam_init = linit_ref[...]
    lam = (jnp.exp(jnp.sum(lam_ref[0:1, :] * lam_ref[1:2, :], axis=1, keepdims=True))
           - jnp.exp(jnp.sum(lam_ref[2:3, :] * lam_ref[3:4, :], axis=1, keepdims=True)) + lam_init)
    gsub = gsub_ref[...] * (1.0 - lam_init)
    gsub = jnp.concatenate([gsub] * (TQ // LANES), axis=1)
    seq = k_ref.shape[1]
    q = q_ref[...]
    maps = [(lambda c, g=g: k_ref[g, _chunk(c), :], q[:, LANES * (g // 4):LANES * (g // 4 + 1)],
             lambda c, g=g: v_ref[g // 2, :, _chunk(c)]) for g in range(DIFF_GROUPS)]
    outs = _attend_t(maps, seq, DIFF_V_DIM)
    normed = []
    for hd in range(DIFF_HEADS):
        o = outs[2 * hd] - lam * outs[2 * hd + 1]
        ms = jnp.mean(o * o, axis=0, keepdims=True)
        normed.append(o * lax.rsqrt(ms + EPS) * gsub)
    for pair in range(DIFF_HEADS // 2):
        o_t = jnp.concatenate(normed[2 * pair:2 * pair + 2], axis=0)
        o_ref[:, LANES * pair:LANES * (pair + 1)] = o_t.T.astype(BF16)


def _diff_attn(lidx, p, qd, kd, vd, batch, seq):
    t = qd.shape[0]
    nq = seq // TQ
    return _call(
        _diff_attn_kernel, "diff_attn", (batch, nq),
        in_specs=[_layer(4, DIFF_QK_DIM), _layer(1, 1), _layer(DIFF_V_DIM, LANES),
                  pl.BlockSpec((TQ, 256), lambda b, i, l: (b * nq + i, 0)),
                  pl.BlockSpec((DIFF_GROUPS, seq, LANES), lambda b, i, l: (0, b, 0)),
                  pl.BlockSpec((DIFF_HEADS, VT_ROWS, seq), lambda b, i, l: (0, 0, b))],
        out_specs=pl.BlockSpec((TQ, 256), lambda b, i, l: (b * nq + i, 0)),
        out_shape=jax.ShapeDtypeStruct((t, DIFF_HEADS * DIFF_V_DIM), BF16),
        semantics=("parallel", "arbitrary"),
    )(lidx, p["lam"], p["lam_init"], p["g_sub"], qd, kd, vd)


def _mix_mem_kernel(_, x_ref, oa_ref, ob_ref, oc_ref, wmix_ref, gmix_ref, gpre_ref, wq_ref,
                    k_ref, v_ref, wo_ref, gpost_ref, o_ref):
    cat = jnp.concatenate([oa_ref[...], ob_ref[...], oc_ref[...]], axis=1)
    x1 = x_ref[...] + _rms(_dot(cat, wmix_ref[...]), gmix_ref[...])
    hq = _rms(x1, gpre_ref[...]).astype(BF16)
    q = _dot(hq, wq_ref[...]).astype(BF16)
    c = (X_HEAD_DIM ** -0.5) * LOG2E
    heads = []
    for hd in range(X_HEADS):
        cols = slice(X_HEAD_DIM * hd, X_HEAD_DIM * (hd + 1))
        s = _dot_nt(q[:, cols], k_ref[:, cols])
        m = jnp.max(s, axis=1, keepdims=True)
        e = jnp.exp2((s - m) * c)
        den = jnp.sum(e, axis=1, keepdims=True)
        heads.append((_dot(e.astype(BF16), v_ref[:, cols]) / den).astype(BF16))
    o = _dot(jnp.concatenate(heads, axis=1), wo_ref[...])
    o_ref[...] = x1 + _rms(o, gpost_ref[...])


def _mix_mem(lidx, x, oa, ob, oc, p, kv, seq):
    t = x.shape[0]
    tm = TM_MEM
    per_b = seq // tm
    return _call(
        _mix_mem_kernel, "mix_mem", (t // tm,),
        in_specs=[_rows(tm, D_MODEL), _rows(tm, MLA_HEADS * MLA_V), _rows(tm, SGU_WIDTH),
                  _rows(tm, DIFF_HEADS * DIFF_V_DIM),
                  _layer(D_MODEL, D_MODEL), _layer(1, D_MODEL), _layer(1, D_MODEL), _layer(D_MODEL, D_MODEL),
                  pl.BlockSpec((None, N_MEM, D_MODEL), lambda i, l: (l[0], i // per_b, 0)),
                  pl.BlockSpec((None, N_MEM, D_MODEL), lambda i, l: (l[0], i // per_b, 1)),
                  _layer(D_MODEL, D_MODEL), _layer(1, D_MODEL)],
        out_specs=_rows(tm, D_MODEL),
        out_shape=jax.ShapeDtypeStruct((t, D_MODEL), F32),
        semantics=("parallel",),
    )(lidx, x, oa, ob, oc, p["w_mix"], p["g_mix_post"], p["g_mem_pre"], p["w_mq"], kv, kv,
      p["w_mo"], p["g_mem_post"])


def _ffn_up_kernel(_, x_ref, xp_ref, xn_ref, g_ref, w_ref, cw_ref, cb_ref, o_ref, h_ref, *, seq):
    tm = x_ref.shape[0]
    i = pl.program_id(0)

    @pl.when(pl.program_id(1) == 0)
    def _():
        g = g_ref[...]
        at_start = (i * tm) % seq == 0
        at_end = ((i + 1) * tm) % seq == 0
        h_ref[0:FFN_HALO, :] = jnp.where(at_start, 0.0, _rms(xp_ref[...], g)).astype(BF16)
        h_ref[FFN_HALO:FFN_HALO + tm, :] = _rms(x_ref[...], g).astype(BF16)
        h_ref[FFN_HALO + tm:, :] = jnp.where(at_end, 0.0, _rms(xn_ref[...], g)).astype(BF16)

    rows = tm + 2 * FFN_HALO
    tn = o_ref.shape[1]
    a = _dot(h_ref[...], w_ref[...])
    cw = cw_ref[...]
    c = cw[0:1] * pltpu.roll(a, 1, 0) + cw[1:2] * a + cw[2:3] * pltpu.roll(a, rows - 1, 0)
    c = c[FFN_HALO:FFN_HALO + tm] + cb_ref[...]
    o_ref[...] = (_gelu(c[:, :tn]) * c[:, tn:]).astype(BF16)


def _ffn_up(lidx, x, p, seq):
    t = x.shape[0]
    tm = TM_FFN
    tn = FFN_TN
    nj = D_FF // tn
    hb = tm // FFN_HALO
    last = t // FFN_HALO - 1
    return _call(
        functools.partial(_ffn_up_kernel, seq=seq), "ffn_up", (t // tm, nj),
        in_specs=[pl.BlockSpec((tm, D_MODEL), lambda i, j, l: (i, 0)),
                  pl.BlockSpec((FFN_HALO, D_MODEL), lambda i, j, l: (jnp.maximum(i * hb - 1, 0), 0)),
                  pl.BlockSpec((FFN_HALO, D_MODEL), lambda i, j, l: (jnp.minimum((i + 1) * hb, last), 0)),
                  _layer(1, D_MODEL),
                  pl.BlockSpec((None, D_MODEL, 2 * tn), lambda i, j, l: (l[0], 0, j)),
                  pl.BlockSpec((None, CONV_WIDTH, 2 * tn), lambda i, j, l: (l[0], 0, j)),
                  pl.BlockSpec((None, 1, 2 * tn), lambda i, j, l: (l[0], 0, j))],
        out_specs=pl.BlockSpec((tm, tn), lambda i, j, l: (i, j)),
        out_shape=jax.ShapeDtypeStruct((t, D_FF), BF16),
        semantics=("parallel", "arbitrary"),
        scratch=[pltpu.VMEM((tm + 2 * FFN_HALO, D_MODEL), BF16)],
    )(lidx, x, x, x, p["g_ffn_pre"], p["w_up"], p["conv_w"], p["conv_b"])


def _ffn_down_kernel(_, x_ref, a_ref, w_ref, g_ref, o_ref):
    o_ref[...] = x_ref[...] + _rms(_dot(a_ref[...], w_ref[...]), g_ref[...])


def _ffn_down(lidx, x, a, p):
    t = x.shape[0]
    tm = TM_FFN
    return _call(
        _ffn_down_kernel, "ffn_down", (t // tm,),
        in_specs=[_rows(tm, D_MODEL), _rows(tm, D_FF), _layer(D_FF, D_MODEL), _layer(1, D_MODEL)],
        out_specs=_rows(tm, D_MODEL),
        out_shape=jax.ShapeDtypeStruct((t, D_MODEL), F32),
        semantics=("parallel",),
    )(lidx, x, a, p["w_down"], p["g_ffn_post"])


def _take_cols(w, idx):
    cols = jnp.take(w, jnp.asarray(np.maximum(idx, 0)), axis=-1)
    return jnp.where(jnp.asarray(idx >= 0), cols, jnp.zeros((), w.dtype))


def _pair_cols(a):
    blocks = []
    for j in range(D_FF // FFN_TN):
        blocks += [a[..., j * FFN_TN:(j + 1) * FFN_TN], a[..., D_FF + j * FFN_TN:D_FF + (j + 1) * FFN_TN]]
    return jnp.concatenate(blocks, axis=-1)


def _mla_head_lanes():
    src = np.full((HEAD_PAD,), -1, np.int64)
    src[0:16] = MLA_NOPE + np.arange(16)
    src[16:64] = np.arange(48)
    src[64:80] = MLA_NOPE + 16 + np.arange(16)
    src[80:96] = 48 + np.arange(16)
    return src


def _diff_lanes():
    src = np.zeros((256,), np.int64)
    for g in range(DIFF_GROUPS):
        base = g * DIFF_QK_DIM
        lane0 = LANES * (g // 4) + DIFF_ROT * (g % 4)
        for i in range(DIFF_ROT):
            src[lane0 + i] = base + i
            src[lane0 + 64 + i] = base + DIFF_ROT + i
            src[lane0 + 32 + i] = base + 2 * DIFF_ROT + i
            src[lane0 + 96 + i] = base + 3 * DIFF_ROT + i
    return src


def _layouts():
    head = _mla_head_lanes()
    p_cq, p_ckv, p_kr, p_sgu = MLA_Q_LORA, MLA_KV_LORA, MLA_ROPE, 2 * SGU_WIDTH
    o_kr = p_cq + p_ckv
    o_z = o_kr + p_kr
    o_dq = o_z + p_sgu
    o_dk = o_dq + 256
    o_dv = o_dk + 256
    kr_src = np.where(head >= MLA_NOPE, head - MLA_NOPE, -1)
    dl = _diff_lanes()

    def win_layout(w):
        out = jnp.concatenate([w[..., :o_kr], _take_cols(w[..., o_kr:o_z], kr_src), w[..., o_z:o_dq],
                               _take_cols(w[..., o_dq:o_dk], dl), _take_cols(w[..., o_dk:o_dv], dl),
                               w[..., o_dv:]], axis=-1)
        assert out.shape[-1] == IN_PAD
        return out
    per_q = MLA_NOPE + MLA_ROPE
    wuq_idx = np.concatenate([np.where(head >= 0, h * per_q + head, -1) for h in range(MLA_HEADS)])
    per_kv = MLA_NOPE + MLA_V
    k_src = np.where((head >= 0) & (head < MLA_NOPE), head, -1)
    wuk_idx = np.concatenate([np.where(k_src >= 0, h * per_kv + k_src, -1) for h in range(MLA_HEADS)])
    wuv_idx = np.concatenate([h * per_kv + MLA_NOPE + np.arange(MLA_V) for h in range(MLA_HEADS)])
    inv_a = ROPE_THETA ** (-jnp.arange(0, MLA_ROPE, 2, dtype=F32) / MLA_ROPE)
    inv_d = ROPE_THETA ** (-jnp.arange(0, 2 * DIFF_ROT, 2, dtype=F32) / (2 * DIFF_ROT))
    fa = jnp.zeros((LANES,), F32).at[0:16].set(inv_a).at[64:80].set(inv_a)
    ga = jnp.zeros((LANES,), F32).at[0:16].set(-1.0).at[64:80].set(1.0)
    fd = jnp.zeros((LANES,), F32).at[0:32].set(jnp.tile(inv_d, 4)).at[64:96].set(jnp.tile(inv_d, 4))
    gd = jnp.zeros((LANES,), F32).at[0:32].set(-1.0).at[64:96].set(1.0)
    return win_layout, wuq_idx, wuk_idx, wuv_idx, fa[None], ga[None], fd[None], gd[None]


def kernel(x, mem, positions, mix_pre_g, mix_post_g, w_in, mla_cq_g, mla_ckv_g, mla_w_uq, mla_w_ukv,
           sgu_norm_g, sgu_w_s, sgu_b_s, diff_lam_q1, diff_lam_k1, diff_lam_q2, diff_lam_k2, diff_sub_g,
           w_mix_out, mem_pre_g, mem_post_g, mem_kv_g, mem_w_q, mem_w_kv, mem_w_o,
           ffn_pre_g, ffn_post_g, ffn_w_up, ffn_conv_w, ffn_conv_b, ffn_w_down):
    batch, seq, d = x.shape
    t = batch * seq
    assert d == D_MODEL and seq % TM_FFN == 0 and seq % TQ == 0 and seq % SGU_CHUNK == 0
    win_layout, wuq_idx, wuk_idx, wuv_idx, fa, ga, fd, gd = _layouts()

    ca, sa, cd, sd = _rope_tables(positions.reshape(t, 1), fa, ga, fd, gd)

    vec = lambda a: a.astype(F32)[:, None, :]
    lam_init = np.array([0.8 - 0.6 * math.exp(-0.3 * l) for l in range(DEPTH)], np.float32)
    w_ukv16 = mla_w_ukv.astype(BF16)
    p = dict(
        g_mix_pre=vec(mix_pre_g), g_mix_post=vec(mix_post_g),
        w_in=win_layout(w_in.astype(BF16)),
        g_cq=vec(mla_cq_g), g_ckv=vec(mla_ckv_g),
        w_uq=_take_cols(mla_w_uq.astype(BF16), wuq_idx),
        w_uk=_take_cols(w_ukv16, wuk_idx),
        w_uv=_take_cols(w_ukv16, wuv_idx),
        g_sgu=vec(sgu_norm_g.reshape(DEPTH, SGU_WIDTH)),
        w_s=sgu_w_s.astype(BF16),
        b_s=jnp.repeat(jnp.swapaxes(sgu_b_s.astype(F32), 1, 2), SGU_GROUP_DIM, axis=2),
        lam=jnp.stack([diff_lam_q1, diff_lam_k1, diff_lam_q2, diff_lam_k2], axis=1).astype(F32),
        lam_init=jnp.asarray(lam_init).reshape(DEPTH, 1, 1),
        g_sub=jnp.repeat(diff_sub_g.astype(F32)[:, :, None], LANES, axis=2),
        w_mix=w_mix_out.astype(BF16),
        g_mem_pre=vec(mem_pre_g), g_mem_post=vec(mem_post_g),
        w_mq=mem_w_q.astype(BF16), w_mo=mem_w_o.astype(BF16),
        g_ffn_pre=vec(ffn_pre_g), g_ffn_post=vec(ffn_post_g),
        w_up=_pair_cols(ffn_w_up.astype(BF16)), conv_w=_pair_cols(ffn_conv_w.astype(F32)),
        conv_b=_pair_cols(vec(ffn_conv_b)),
        w_down=ffn_w_down.astype(BF16),
    )
    kv_all = _mem_kv(mem.reshape(batch * N_MEM, D_MODEL), vec(mem_kv_g), mem_w_kv)

    xc = x.reshape(t, D_MODEL)
    for l in range(DEPTH):
        lidx = jnp.full((1,), l, jnp.int32)
        qa, ka, va, sgu, qd, kd, vd = _mix_in(lidx, xc, p, ca, sa, cd, sd)
        out_a = _mla_attn(lidx, qa, ka, va, batch, seq)
        out_c = _diff_attn(lidx, p, qd, kd, vd, batch, seq)
        x2 = _mix_mem(lidx, xc, out_a, sgu, out_c, p, kv_all, seq)
        mid = _ffn_up(lidx, x2, p, seq)
        xc = _ffn_down(lidx, x2, mid, p)
    return xc.reshape(batch, seq, D_MODEL)
```

```python
import functools
import math

import jax
import jax.numpy as jnp
import numpy as np
from jax import lax
from jax.experimental import pallas as pl
from jax.experimental.pallas import tpu as pltpu

F32 = jnp.float32
BF16 = jnp.bfloat16

D_MODEL = 1024
DEPTH = 4
N_MEM = 256
ROPE_THETA = 500000.0
EPS = 1e-6

MLA_HEADS = 8
MLA_NOPE = 64
MLA_ROPE = 32
MLA_V = 64
MLA_Q_LORA = 384
MLA_KV_LORA = 256

SGU_GROUPS = 4
SGU_GROUP_DIM = 64
SGU_WIDTH = SGU_GROUPS * SGU_GROUP_DIM
SGU_CHUNK = 128

DIFF_HEADS = 4
DIFF_QK_DIM = 32
DIFF_V_DIM = 64
DIFF_ROT = 8
DIFF_GROUPS = 2 * DIFF_HEADS

X_HEADS = 4
X_HEAD_DIM = D_MODEL // X_HEADS

D_FF = 2816
CONV_WIDTH = 3

LANES = 128
HEAD_PAD = LANES
VT_ROWS = MLA_V + 16
LOG2E = math.log2(math.e)
Q_SCALE_MLA = (MLA_NOPE + MLA_ROPE) ** -0.5 * LOG2E
Q_SCALE_DIFF = DIFF_QK_DIM ** -0.5 * LOG2E
VMEM_LIMIT = 56 * 1024 * 1024

O_CQ = 0
O_CKV = O_CQ + MLA_Q_LORA
O_KR = O_CKV + MLA_KV_LORA
O_Z = O_KR + HEAD_PAD
O_DQ = O_Z + 2 * SGU_WIDTH
O_DK = O_DQ + 256
O_DV = O_DK + 256
IN_PAD = O_DV + 256

TM_IN = 512
IN_PARTS = 4
TQ = 256
KV_CHUNK = 256
ATTN_AHEAD = 6
TM_MEM = 512
TM_KV = 1024
TM_FFN = 512
FFN_HALO = 16
FFN_SPLIT = (0, 1536, D_FF)


def _call(body, name, grid, in_specs, out_specs, out_shape, semantics, scratch=()):
    return pl.pallas_call(
        body,
        grid_spec=pltpu.PrefetchScalarGridSpec(
            num_scalar_prefetch=1, grid=grid, in_specs=in_specs, out_specs=out_specs,
            scratch_shapes=list(scratch)),
        out_shape=out_shape,
        compiler_params=pltpu.CompilerParams(dimension_semantics=semantics, vmem_limit_bytes=VMEM_LIMIT),
        name=name)


def _rows(tm, width):
    return pl.BlockSpec((tm, width), lambda i, l: (i, 0))


def _layer(*tail):
    return pl.BlockSpec((None,) + tail, lambda *a: (a[-1][0],) + (0,) * len(tail))


def _rms(x, g):
    ms = jnp.mean(x * x, axis=-1, keepdims=True)
    return x * lax.rsqrt(ms + EPS) * g


def _gelu(x):
    c = math.sqrt(2.0 / math.pi)
    return 0.5 * x * (1.0 + jnp.tanh(c * (x + 0.044715 * (x * x * x))))


def _dot(a, b):
    return jnp.dot(a, b, preferred_element_type=F32)


def _dot_nt(a, b):
    return lax.dot_general(a, b, (((1,), (1,)), ((), ())), preferred_element_type=F32)


def _rope_kernel(pos_ref, fa_ref, ga_ref, fd_ref, gd_ref, ca_ref, sa_ref, cd_ref, sd_ref):
    pos = pos_ref[...].astype(F32)
    ang_a = pos * fa_ref[...]
    ca_ref[...] = jnp.cos(ang_a)
    sa_ref[...] = jnp.sin(ang_a) * ga_ref[...]
    ang_d = pos * fd_ref[...]
    cd_ref[...] = jnp.cos(ang_d)
    sd_ref[...] = jnp.sin(ang_d) * gd_ref[...]


def _rope_tables(pos, fa, ga, fd, gd):
    t = pos.shape[0]
    tm = 1024
    row = lambda w: pl.BlockSpec((tm, w), lambda i: (i, 0))
    vec = lambda w: pl.BlockSpec((1, w), lambda i: (0, 0))
    return pl.pallas_call(
        _rope_kernel,
        grid=(t // tm,),
        in_specs=[row(1), vec(LANES), vec(LANES), vec(LANES), vec(LANES)],
        out_specs=[row(LANES)] * 4,
        out_shape=[jax.ShapeDtypeStruct((t, LANES), F32)] * 4,
        compiler_params=pltpu.CompilerParams(dimension_semantics=("parallel",), vmem_limit_bytes=VMEM_LIMIT),
        name="rope_tables",
    )(pos, fa, ga, fd, gd)


def _mem_kv_kernel(mem_ref, g_ref, w_ref, o_ref, w16_ref):
    @pl.when(pl.program_id(1) == 0)
    def _():
        w16_ref[...] = w_ref[...].astype(BF16)

    hn = _rms(mem_ref[...], g_ref[...]).astype(BF16)
    o_ref[...] = _dot(hn, w16_ref[...]).astype(BF16)


def _mem_kv(mem2, g, w):
    rows = mem2.shape[0]
    tm = TM_KV
    return pl.pallas_call(
        _mem_kv_kernel,
        grid=(DEPTH, rows // tm),
        in_specs=[pl.BlockSpec((tm, D_MODEL), lambda l, i: (i, 0)),
                  pl.BlockSpec((None, 1, D_MODEL), lambda l, i: (l, 0, 0)),
                  pl.BlockSpec((None, D_MODEL, 2 * D_MODEL), lambda l, i: (l, 0, 0))],
        out_specs=pl.BlockSpec((None, tm, 2 * D_MODEL), lambda l, i: (l, i, 0)),
        out_shape=jax.ShapeDtypeStruct((DEPTH, rows, 2 * D_MODEL), BF16),
        scratch_shapes=[pltpu.VMEM((D_MODEL, 2 * D_MODEL), BF16)],
        compiler_params=pltpu.CompilerParams(dimension_semantics=("arbitrary", "arbitrary"),
                                             vmem_limit_bytes=VMEM_LIMIT),
        name="mem_kv",
    )(mem2, g, w)


def _mix_in_kernel(_, x_ref, g_ref, win_ref, gcq_ref, gckv_ref, wuq_ref, wuk_ref, wuv_ref,
                   ca_ref, sa_ref, cd_ref, sd_ref, gsgu_ref, ws_ref, bs_ref,
                   qa_ref, ka_ref, va_ref, sgu_ref, qd_ref, kd_ref, vd_ref):
    tm = x_ref.shape[0]
    part = tm // IN_PARTS
    parts = [pl.ds(r, part) for r in range(0, tm, part)]
    projs = [_dot(_rms(x_ref[r, :], g_ref[...]).astype(BF16), win_ref[...]) for r in parts]
    for r, proj in zip(parts, projs):
        _mix_in_part(r, proj, gcq_ref, gckv_ref, wuq_ref, wuk_ref, wuv_ref, ca_ref, sa_ref, cd_ref, sd_ref,
                     gsgu_ref, ws_ref, bs_ref, qa_ref, ka_ref, va_ref, sgu_ref, qd_ref, kd_ref, vd_ref)


def _mix_in_part(r, proj, gcq_ref, gckv_ref, wuq_ref, wuk_ref, wuv_ref, ca_ref, sa_ref, cd_ref, sd_ref,
                 gsgu_ref, ws_ref, bs_ref, qa_ref, ka_ref, va_ref, sgu_ref, qd_ref, kd_ref, vd_ref):
    tm = proj.shape[0]
    lane = lax.broadcasted_iota(jnp.int32, (tm, LANES), 1)
    low_half = lane < 64

    ca = ca_ref[r, :]
    sa = sa_ref[r, :]
    cqn = _rms(proj[:, O_CQ:O_CQ + MLA_Q_LORA], gcq_ref[...]).astype(BF16)
    q = _dot(cqn, wuq_ref[...])
    ckvn = _rms(proj[:, O_CKV:O_CKV + MLA_KV_LORA], gckv_ref[...]).astype(BF16)
    kn = _dot(ckvn, wuk_ref[...])
    vv = _dot(ckvn, wuv_ref[...])
    kr = proj[:, O_KR:O_KR + HEAD_PAD]
    kr = kr * ca + pltpu.roll(kr, 64, 1) * sa
    vv_t = vv.T
    ones_row = (lax.broadcasted_iota(jnp.int32, (VT_ROWS - MLA_V, tm), 0) == 0).astype(BF16)
    for hd in range(MLA_HEADS):
        qh = q[:, HEAD_PAD * hd:HEAD_PAD * (hd + 1)]
        qa_ref[hd, r, :] = ((qh * ca + pltpu.roll(qh, 64, 1) * sa) * Q_SCALE_MLA).astype(BF16)
        ka_ref[hd, r, :] = (kn[:, HEAD_PAD * hd:HEAD_PAD * (hd + 1)] + kr).astype(BF16)
        va_ref[hd, 0:MLA_V, r] = vv_t[MLA_V * hd:MLA_V * (hd + 1), :].astype(BF16)
        va_ref[hd, MLA_V:VT_ROWS, r] = ones_row

    zg = _gelu(proj[:, O_Z:O_Z + 2 * SGU_WIDTH])
    u = zg[:, :SGU_WIDTH]
    gsgu = gsgu_ref[...]
    inv_gd = 1.0 / SGU_GROUP_DIM
    vn_tiles = []
    for t in range(SGU_WIDTH // LANES):
        vt = zg[:, SGU_WIDTH + LANES * t:SGU_WIDTH + LANES * (t + 1)]
        s_lo = jnp.sum(jnp.where(low_half, vt, 0.0), axis=1, keepdims=True)
        s_hi = jnp.sum(jnp.where(low_half, 0.0, vt), axis=1, keepdims=True)
        xc = vt - jnp.where(low_half, s_lo, s_hi) * inv_gd
        sq = xc * xc
        v_lo = jnp.sum(jnp.where(low_half, sq, 0.0), axis=1, keepdims=True)
        v_hi = jnp.sum(jnp.where(low_half, 0.0, sq), axis=1, keepdims=True)
        var = jnp.where(low_half, v_lo, v_hi) * inv_gd
        vn_tiles.append((xc * lax.rsqrt(var + EPS) * gsgu[:, LANES * t:LANES * (t + 1)]).astype(BF16))
    bias = bs_ref[...]
    low_chunk = lax.broadcasted_iota(jnp.int32, (SGU_CHUNK, LANES), 1) < 64
    for n in range(tm // SGU_CHUNK):
        rows = slice(SGU_CHUNK * n, SGU_CHUNK * (n + 1))
        out_rows = pl.ds(r.start + SGU_CHUNK * n, SGU_CHUNK)
        for t in range(SGU_WIDTH // LANES):
            vc = vn_tiles[t][rows, :]
            m_lo = _dot(ws_ref[2 * t], vc)
            m_hi = _dot(ws_ref[2 * t + 1], vc)
            mixed = jnp.where(low_chunk, m_lo, m_hi) + bias[:, LANES * t:LANES * (t + 1)]
            sgu_ref[out_rows, LANES * t:LANES * (t + 1)] = (u[rows, LANES * t:LANES * (t + 1)] * mixed).astype(BF16)

    cd = cd_ref[r, :]
    sd = sd_ref[r, :]
    group = (lane & 31) >> 3
    for t in range(2):
        dq = proj[:, O_DQ + LANES * t:O_DQ + LANES * (t + 1)]
        qd_ref[r, LANES * t:LANES * (t + 1)] = (
            (dq * cd + pltpu.roll(dq, 64, 1) * sd) * Q_SCALE_DIFF).astype(BF16)
        dk = proj[:, O_DK + LANES * t:O_DK + LANES * (t + 1)]
        dk = dk * cd + pltpu.roll(dk, 64, 1) * sd
        for gl in range(DIFF_GROUPS // 2):
            kd_ref[4 * t + gl, r, :] = jnp.where(group == gl, dk, 0.0).astype(BF16)
    dv_t = proj[:, O_DV:O_DV + 256].T
    for hd in range(DIFF_HEADS):
        vd_ref[hd, 0:DIFF_V_DIM, r] = dv_t[DIFF_V_DIM * hd:DIFF_V_DIM * (hd + 1), :].astype(BF16)
        vd_ref[hd, DIFF_V_DIM:VT_ROWS, r] = ones_row


def _mix_in(lidx, x, p, ca, sa, cd, sd):
    t = x.shape[0]
    tm = TM_IN
    heads = lambda n, w: pl.BlockSpec((n, tm, w), lambda i, l: (0, i, 0))
    return _call(
        _mix_in_kernel, "mix_in", (t // tm,),
        in_specs=[_rows(tm, D_MODEL), _layer(1, D_MODEL), _layer(D_MODEL, IN_PAD),
                  _layer(1, MLA_Q_LORA), _layer(1, MLA_KV_LORA),
                  _layer(MLA_Q_LORA, MLA_HEADS * HEAD_PAD), _layer(MLA_KV_LORA, MLA_HEADS * HEAD_PAD),
                  _layer(MLA_KV_LORA, MLA_HEADS * MLA_V),
                  _rows(tm, LANES), _rows(tm, LANES), _rows(tm, LANES), _rows(tm, LANES),
                  _layer(1, SGU_WIDTH), _layer(SGU_GROUPS, SGU_CHUNK, SGU_CHUNK), _layer(SGU_CHUNK, SGU_WIDTH)],
        out_specs=[heads(MLA_HEADS, HEAD_PAD), heads(MLA_HEADS, HEAD_PAD),
                   pl.BlockSpec((MLA_HEADS, VT_ROWS, tm), lambda i, l: (0, 0, i)),
                   _rows(tm, SGU_WIDTH), _rows(tm, 256), heads(DIFF_GROUPS, LANES),
                   pl.BlockSpec((DIFF_HEADS, VT_ROWS, tm), lambda i, l: (0, 0, i))],
        out_shape=[jax.ShapeDtypeStruct((MLA_HEADS, t, HEAD_PAD), BF16),
                   jax.ShapeDtypeStruct((MLA_HEADS, t, HEAD_PAD), BF16),
                   jax.ShapeDtypeStruct((MLA_HEADS, VT_ROWS, t), BF16),
                   jax.ShapeDtypeStruct((t, SGU_WIDTH), BF16),
                   jax.ShapeDtypeStruct((t, 256), BF16),
                   jax.ShapeDtypeStruct((DIFF_GROUPS, t, LANES), BF16),
                   jax.ShapeDtypeStruct((DIFF_HEADS, VT_ROWS, t), BF16)],
        semantics=("parallel",),
    )(lidx, x, p["g_mix_pre"], p["w_in"], p["g_cq"], p["g_ckv"], p["w_uq"], p["w_uk"], p["w_uv"],
      ca, sa, cd, sd, p["g_sgu"], p["w_s"], p["b_s"])


def _attend_t(maps, seq, dv):
    n_chunks = seq // KV_CHUNK
    items = [(i, c) for i in range(len(maps)) for c in range(n_chunks)]
    scores = lambda item: _dot_nt(maps[item[0]][0](item[1]), maps[item[0]][1])
    pending = [scores(item) for item in items[:ATTN_AHEAD]]
    outs = []
    m = acc = None
    for n, (i, c) in enumerate(items):
        if n + ATTN_AHEAD < len(items):
            pending.append(scores(items[n + ATTN_AHEAD]))
        s = pending.pop(0)
        mc = jnp.max(s, axis=0, keepdims=True)
        m_new = mc if c == 0 else jnp.maximum(m, mc)
        e = jnp.exp2(s - m_new).astype(BF16)
        r = _dot(maps[i][2](c), e)
        acc = r if c == 0 else acc * jnp.exp2(m - m_new) + r
        m = m_new
        if c == n_chunks - 1:
            outs.append(acc[0:dv] / acc[dv:dv + 1])
    return outs


def _chunk(c):
    return pl.ds(c * KV_CHUNK, KV_CHUNK)


def _mla_attn_kernel(_, q_ref, k_ref, v_ref, o_ref):
    seq = k_ref.shape[1]
    maps = [(lambda c, hd=hd: k_ref[hd, _chunk(c), :], q_ref[hd], lambda c, hd=hd: v_ref[hd, :, _chunk(c)])
            for hd in range(MLA_HEADS)]
    outs = _attend_t(maps, seq, MLA_V)
    for pair in range(MLA_HEADS // 2):
        o_t = jnp.concatenate(outs[2 * pair:2 * pair + 2], axis=0)
        o_ref[:, LANES * pair:LANES * (pair + 1)] = o_t.T.astype(BF16)


def _mla_attn(lidx, qa, ka, va, batch, seq):
    t = qa.shape[1]
    nq = seq // TQ
    return _call(
        _mla_attn_kernel, "mla_attn", (batch, nq),
        in_specs=[pl.BlockSpec((MLA_HEADS, TQ, HEAD_PAD), lambda b, i, l: (0, b * nq + i, 0)),
                  pl.BlockSpec((MLA_HEADS, seq, HEAD_PAD), lambda b, i, l: (0, b, 0)),
                  pl.BlockSpec((MLA_HEADS, VT_ROWS, seq), lambda b, i, l: (0, 0, b))],
        out_specs=pl.BlockSpec((TQ, MLA_HEADS * MLA_V), lambda b, i, l: (b * nq + i, 0)),
        out_shape=jax.ShapeDtypeStruct((t, MLA_HEADS * MLA_V), BF16),
        semantics=("parallel", "arbitrary"),
    )(lidx, qa, ka, va)


def _diff_attn_kernel(_, lam_ref, linit_ref, gsub_ref, q_ref, k_ref, v_ref, o_ref):
    lam_init = linit_ref[...]
    lam = (jnp.exp(jnp.sum(lam_ref[0:1, :] * lam_ref[1:2, :], axis=1, keepdims=True))
           - jnp.exp(jnp.sum(lam_ref[2:3, :] * lam_ref[3:4, :], axis=1, keepdims=True)) + lam_init)
    gsub = gsub_ref[...] * (1.0 - lam_init)
    gsub = jnp.concatenate([gsub] * (TQ // LANES), axis=1)
    seq = k_ref.shape[1]
    q = q_ref[...]
    maps = [(lambda c, g=g: k_ref[g, _chunk(c), :], q[:, LANES * (g // 4):LANES * (g // 4 + 1)],
             lambda c, g=g: v_ref[g // 2, :, _chunk(c)]) for g in range(DIFF_GROUPS)]
    outs = _attend_t(maps, seq, DIFF_V_DIM)
    normed = []
    for hd in range(DIFF_HEADS):
        o = outs[2 * hd] - lam * outs[2 * hd + 1]
        ms = jnp.mean(o * o, axis=0, keepdims=True)
        normed.append(o * lax.rsqrt(ms + EPS) * gsub)
    for pair in range(DIFF_HEADS // 2):
        o_t = jnp.concatenate(normed[2 * pair:2 * pair + 2], axis=0)
        o_ref[:, LANES * pair:LANES * (pair + 1)] = o_t.T.astype(BF16)


def _diff_attn(lidx, p, qd, kd, vd, batch, seq):
    t = qd.shape[0]
    nq = seq // TQ
    return _call(
        _diff_attn_kernel, "diff_attn", (batch, nq),
        in_specs=[_layer(4, DIFF_QK_DIM), _layer(1, 1), _layer(DIFF_V_DIM, LANES),
                  pl.BlockSpec((TQ, 256), lambda b, i, l: (b * nq + i, 0)),
                  pl.BlockSpec((DIFF_GROUPS, seq, LANES), lambda b, i, l: (0, b, 0)),
                  pl.BlockSpec((DIFF_HEADS, VT_ROWS, seq), lambda b, i, l: (0, 0, b))],
        out_specs=pl.BlockSpec((TQ, 256), lambda b, i, l: (b * nq + i, 0)),
        out_shape=jax.ShapeDtypeStruct((t, DIFF_HEADS * DIFF_V_DIM), BF16),
        semantics=("parallel", "arbitrary"),
    )(lidx, p["lam"], p["lam_init"], p["g_sub"], qd, kd, vd)


def _mix_mem_kernel(_, x_ref, oa_ref, ob_ref, oc_ref, wmix_ref, gmix_ref, gpre_ref, wq_ref,
                    k_ref, v_ref, wo_ref, gpost_ref, o_ref):
    cat = jnp.concatenate([oa_ref[...], ob_ref[...], oc_ref[...]], axis=1)
    x1 = x_ref[...] + _rms(_dot(cat, wmix_ref[...]), gmix_ref[...])
    hq = _rms(x1, gpre_ref[...]).astype(BF16)
    q = _dot(hq, wq_ref[...]).astype(BF16)
    c = (X_HEAD_DIM ** -0.5) * LOG2E
    heads = []
    for hd in range(X_HEADS):
        cols = slice(X_HEAD_DIM * hd, X_HEAD_DIM * (hd + 1))
        s = _dot_nt(q[:, cols], k_ref[:, cols])
        m = jnp.max(s, axis=1, keepdims=True)
        e = jnp.exp2((s - m) * c)
        den = jnp.sum(e, axis=1, keepdims=True)
        heads.append((_dot(e.astype(BF16), v_ref[:, cols]) / den).astype(BF16))
    o = _dot(jnp.concatenate(heads, axis=1), wo_ref[...])
    o_ref[...] = x1 + _rms(o, gpost_ref[...])


def _mix_mem(lidx, x, oa, ob, oc, p, kv, seq):
    t = x.shape[0]
    tm = TM_MEM
    per_b = seq // tm
    return _call(
        _mix_mem_kernel, "mix_mem", (t // tm,),
        in_specs=[_rows(tm, D_MODEL), _rows(tm, MLA_HEADS * MLA_V), _rows(tm, SGU_WIDTH),
                  _rows(tm, DIFF_HEADS * DIFF_V_DIM),
                  _layer(D_MODEL, D_MODEL), _layer(1, D_MODEL), _layer(1, D_MODEL), _layer(D_MODEL, D_MODEL),
                  pl.BlockSpec((None, N_MEM, D_MODEL), lambda i, l: (l[0], i // per_b, 0)),
                  pl.BlockSpec((None, N_MEM, D_MODEL), lambda i, l: (l[0], i // per_b, 1)),
                  _layer(D_MODEL, D_MODEL), _layer(1, D_MODEL)],
        out_specs=_rows(tm, D_MODEL),
        out_shape=jax.ShapeDtypeStruct((t, D_MODEL), F32),
        semantics=("parallel",),
    )(lidx, x, oa, ob, oc, p["w_mix"], p["g_mix_post"], p["g_mem_pre"], p["w_mq"], kv, kv,
      p["w_mo"], p["g_mem_post"])


def _ffn_up_kernel(_, x_ref, xp_ref, xn_ref, g_ref, w_ref, cw_ref, cb_ref, o_ref, h_ref, *, seq):
    tm = x_ref.shape[0]
    i = pl.program_id(0)
    g = g_ref[...]
    at_start = (i * tm) % seq == 0
    at_end = ((i + 1) * tm) % seq == 0
    h_ref[0:FFN_HALO, :] = jnp.where(at_start, 0.0, _rms(xp_ref[...], g)).astype(BF16)
    h_ref[FFN_HALO:FFN_HALO + tm, :] = _rms(x_ref[...], g).astype(BF16)
    h_ref[FFN_HALO + tm:, :] = jnp.where(at_end, 0.0, _rms(xn_ref[...], g)).astype(BF16)
    hb = h_ref[...]
    rows = tm + 2 * FFN_HALO

    def project(cols):
        up_cols = slice(D_FF + cols.start, D_FF + cols.stop)
        return _dot(hb, w_ref[:, cols]), _dot(hb, w_ref[:, up_cols])

    def conv(a, cols):
        c = (cw_ref[0:1, cols] * pltpu.roll(a, 1, 0) + cw_ref[1:2, cols] * a
             + cw_ref[2:3, cols] * pltpu.roll(a, rows - 1, 0))
        return c[FFN_HALO:FFN_HALO + tm] + cb_ref[:, cols]

    blocks = [slice(lo, hi) for lo, hi in zip(FFN_SPLIT[:-1], FFN_SPLIT[1:])]
    acts = [project(cols) for cols in blocks]
    for cols, (a_gate, a_up) in zip(blocks, acts):
        up_cols = slice(D_FF + cols.start, D_FF + cols.stop)
        o_ref[:, cols] = (_gelu(conv(a_gate, cols)) * conv(a_up, up_cols)).astype(BF16)


def _ffn_up(lidx, x, p, seq):
    t = x.shape[0]
    tm = TM_FFN
    hb = tm // FFN_HALO
    last = t // FFN_HALO - 1
    return _call(
        functools.partial(_ffn_up_kernel, seq=seq), "ffn_up", (t // tm,),
        in_specs=[_rows(tm, D_MODEL),
                  pl.BlockSpec((FFN_HALO, D_MODEL), lambda i, l: (jnp.maximum(i * hb - 1, 0), 0)),
                  pl.BlockSpec((FFN_HALO, D_MODEL), lambda i, l: (jnp.minimum((i + 1) * hb, last), 0)),
                  _layer(1, D_MODEL), _layer(D_MODEL, 2 * D_FF), _layer(CONV_WIDTH, 2 * D_FF), _layer(1, 2 * D_FF)],
        out_specs=_rows(tm, D_FF),
        out_shape=jax.ShapeDtypeStruct((t, D_FF), BF16),
        semantics=("parallel",),
        scratch=[pltpu.VMEM((tm + 2 * FFN_HALO, D_MODEL), BF16)],
    )(lidx, x, x, x, p["g_ffn_pre"], p["w_up"], p["conv_w"], p["conv_b"])


def _ffn_down_kernel(_, x_ref, a_ref, w_ref, g_ref, o_ref):
    o_ref[...] = x_ref[...] + _rms(_dot(a_ref[...], w_ref[...]), g_ref[...])


def _ffn_down(lidx, x, a, p):
    t = x.shape[0]
    tm = TM_FFN
    return _call(
        _ffn_down_kernel, "ffn_down", (t // tm,),
        in_specs=[_rows(tm, D_MODEL), _rows(tm, D_FF), _layer(D_FF, D_MODEL), _layer(1, D_MODEL)],
        out_specs=_rows(tm, D_MODEL),
        out_shape=jax.ShapeDtypeStruct((t, D_MODEL), F32),
        semantics=("parallel",),
    )(lidx, x, a, p["w_down"], p["g_ffn_post"])


def _take_cols(w, idx):
    cols = jnp.take(w, jnp.asarray(np.maximum(idx, 0)), axis=-1)
    return jnp.where(jnp.asarray(idx >= 0), cols, jnp.zeros((), w.dtype))


def _mla_head_lanes():
    src = np.full((HEAD_PAD,), -1, np.int64)
    src[0:16] = MLA_NOPE + np.arange(16)
    src[16:64] = np.arange(48)
    src[64:80] = MLA_NOPE + 16 + np.arange(16)
    src[80:96] = 48 + np.arange(16)
    return src


def _diff_lanes():
    src = np.zeros((256,), np.int64)
    for g in range(DIFF_GROUPS):
        base = g * DIFF_QK_DIM
        lane0 = LANES * (g // 4) + DIFF_ROT * (g % 4)
        for i in range(DIFF_ROT):
            src[lane0 + i] = base + i
            src[lane0 + 64 + i] = base + DIFF_ROT + i
            src[lane0 + 32 + i] = base + 2 * DIFF_ROT + i
            src[lane0 + 96 + i] = base + 3 * DIFF_ROT + i
    return src


def _layouts():
    head = _mla_head_lanes()
    p_cq, p_ckv, p_kr, p_sgu = MLA_Q_LORA, MLA_KV_LORA, MLA_ROPE, 2 * SGU_WIDTH
    o_kr = p_cq + p_ckv
    o_z = o_kr + p_kr
    o_dq = o_z + p_sgu
    o_dk = o_dq + 256
    o_dv = o_dk + 256
    kr_src = np.where(head >= MLA_NOPE, head - MLA_NOPE, -1)
    dl = _diff_lanes()

    def win_layout(w):
        out = jnp.concatenate([w[..., :o_kr], _take_cols(w[..., o_kr:o_z], kr_src), w[..., o_z:o_dq],
                               _take_cols(w[..., o_dq:o_dk], dl), _take_cols(w[..., o_dk:o_dv], dl),
                               w[..., o_dv:]], axis=-1)
        assert out.shape[-1] == IN_PAD
        return out
    per_q = MLA_NOPE + MLA_ROPE
    wuq_idx = np.concatenate([np.where(head >= 0, h * per_q + head, -1) for h in range(MLA_HEADS)])
    per_kv = MLA_NOPE + MLA_V
    k_src = np.where((head >= 0) & (head < MLA_NOPE), head, -1)
    wuk_idx = np.concatenate([np.where(k_src >= 0, h * per_kv + k_src, -1) for h in range(MLA_HEADS)])
    wuv_idx = np.concatenate([h * per_kv + MLA_NOPE + np.arange(MLA_V) for h in range(MLA_HEADS)])
    inv_a = ROPE_THETA ** (-jnp.arange(0, MLA_ROPE, 2, dtype=F32) / MLA_ROPE)
    inv_d = ROPE_THETA ** (-jnp.arange(0, 2 * DIFF_ROT, 2, dtype=F32) / (2 * DIFF_ROT))
    fa = jnp.zeros((LANES,), F32).at[0:16].set(inv_a).at[64:80].set(inv_a)
    ga = jnp.zeros((LANES,), F32).at[0:16].set(-1.0).at[64:80].set(1.0)
    fd = jnp.zeros((LANES,), F32).at[0:32].set(jnp.tile(inv_d, 4)).at[64:96].set(jnp.tile(inv_d, 4))
    gd = jnp.zeros((LANES,), F32).at[0:32].set(-1.0).at[64:96].set(1.0)
    return win_layout, wuq_idx, wuk_idx, wuv_idx, fa[None], ga[None], fd[None], gd[None]


def kernel(x, mem, positions, mix_pre_g, mix_post_g, w_in, mla_cq_g, mla_ckv_g, mla_w_uq, mla_w_ukv,
           sgu_norm_g, sgu_w_s, sgu_b_s, diff_lam_q1, diff_lam_k1, diff_lam_q2, diff_lam_k2, diff_sub_g,
           w_mix_out, mem_pre_g, mem_post_g, mem_kv_g, mem_w_q, mem_w_kv, mem_w_o,
           ffn_pre_g, ffn_post_g, ffn_w_up, ffn_conv_w, ffn_conv_b, ffn_w_down):
    batch, seq, d = x.shape
    t = batch * seq
    assert d == D_MODEL and seq % TM_FFN == 0 and seq % TQ == 0 and seq % SGU_CHUNK == 0
    win_layout, wuq_idx, wuk_idx, wuv_idx, fa, ga, fd, gd = _layouts()

    ca, sa, cd, sd = _rope_tables(positions.reshape(t, 1), fa, ga, fd, gd)

    vec = lambda a: a.astype(F32)[:, None, :]
    lam_init = np.array([0.8 - 0.6 * math.exp(-0.3 * l) for l in range(DEPTH)], np.float32)
    w_ukv16 = mla_w_ukv.astype(BF16)
    p = dict(
        g_mix_pre=vec(mix_pre_g), g_mix_post=vec(mix_post_g),
        w_in=win_layout(w_in.astype(BF16)),
        g_cq=vec(mla_cq_g), g_ckv=vec(mla_ckv_g),
        w_uq=_take_cols(mla_w_uq.astype(BF16), wuq_idx),
        w_uk=_take_cols(w_ukv16, wuk_idx),
        w_uv=_take_cols(w_ukv16, wuv_idx),
        g_sgu=vec(sgu_norm_g.reshape(DEPTH, SGU_WIDTH)),
        w_s=sgu_w_s.astype(BF16),
        b_s=jnp.repeat(jnp.swapaxes(sgu_b_s.astype(F32), 1, 2), SGU_GROUP_DIM, axis=2),
        lam=jnp.stack([diff_lam_q1, diff_lam_k1, diff_lam_q2, diff_lam_k2], axis=1).astype(F32),
        lam_init=jnp.asarray(lam_init).reshape(DEPTH, 1, 1),
        g_sub=jnp.repeat(diff_sub_g.astype(F32)[:, :, None], LANES, axis=2),
        w_mix=w_mix_out.astype(BF16),
        g_mem_pre=vec(mem_pre_g), g_mem_post=vec(mem_post_g),
        w_mq=mem_w_q.astype(BF16), w_mo=mem_w_o.astype(BF16),
        g_ffn_pre=vec(ffn_pre_g), g_ffn_post=vec(ffn_post_g),
        w_up=ffn_w_up.astype(BF16), conv_w=ffn_conv_w.astype(F32), conv_b=vec(ffn_conv_b),
        w_down=ffn_w_down.astype(BF16),
    )
    kv_all = _mem_kv(mem.reshape(batch * N_MEM, D_MODEL), vec(mem_kv_g), mem_w_kv)

    xc = x.reshape(t, D_MODEL)
    for l in range(DEPTH):
        lidx = jnp.full((1,), l, jnp.int32)
        qa, ka, va, sgu, qd, kd, vd = _mix_in(lidx, xc, p, ca, sa, cd, sd)
        out_a = _mla_attn(lidx, qa, ka, va, batch, seq)
        out_c = _diff_attn(lidx, p, qd, kd, vd, batch, seq)
        x2 = _mix_mem(lidx, xc, out_a, sgu, out_c, p, kv_all, seq)
        mid = _ffn_up(lidx, x2, p, seq)
        xc = _ffn_down(lidx, x2, mid, p)
    return xc.reshape(batch, seq, D_MODEL)
```

```python
import functools
import math

import jax
import jax.numpy as jnp
import numpy as np
from jax import lax
from jax.experimental import pallas as pl
from jax.experimental.pallas import tpu as pltpu

F32 = jnp.float32
BF16 = jnp.bfloat16

D_MODEL = 1024
DEPTH = 4
N_MEM = 256
ROPE_THETA = 500000.0
EPS = 1e-6

MLA_HEADS = 8
MLA_NOPE = 64
MLA_ROPE = 32
MLA_V = 64
MLA_Q_LORA = 384
MLA_KV_LORA = 256

SGU_GROUPS = 4
SGU_GROUP_DIM = 64
SGU_WIDTH = SGU_GROUPS * SGU_GROUP_DIM
SGU_CHUNK = 128

DIFF_HEADS = 4
DIFF_QK_DIM = 32
DIFF_V_DIM = 64
DIFF_ROT = 8
DIFF_GROUPS = 2 * DIFF_HEADS

X_HEADS = 4
X_HEAD_DIM = D_MODEL // X_HEADS

D_FF = 2816
CONV_WIDTH = 3

LANES = 128
HEAD_PAD = LANES
VT_ROWS = MLA_V + 16
LOG2E = math.log2(math.e)
Q_SCALE_MLA = (MLA_NOPE + MLA_ROPE) ** -0.5 * LOG2E
Q_SCALE_DIFF = DIFF_QK_DIM ** -0.5 * LOG2E
VMEM_LIMIT = 56 * 1024 * 1024

O_CQ = 0
O_CKV = O_CQ + MLA_Q_LORA
O_KR = O_CKV + MLA_KV_LORA
O_Z = O_KR + HEAD_PAD
O_DQ = O_Z + 2 * SGU_WIDTH
O_DK = O_DQ + 256
O_DV = O_DK + 256
IN_PAD = O_DV + 256

TM_IN = 512
IN_PARTS = 4
TQ = 256
KV_CHUNK = 256
ATTN_AHEAD = 5
TM_MEM = 512
TM_KV = 1024
TM_FFN = 512
TM_FFN_UP = 1024
FFN_HALO = 16
FFN_SPLIT = (0, 1536, D_FF)


def _call(body, name, grid, in_specs, out_specs, out_shape, semantics, scratch=()):
    return pl.pallas_call(
        body,
        grid_spec=pltpu.PrefetchScalarGridSpec(
            num_scalar_prefetch=1, grid=grid, in_specs=in_specs, out_specs=out_specs,
            scratch_shapes=list(scratch)),
        out_shape=out_shape,
        compiler_params=pltpu.CompilerParams(dimension_semantics=semantics, vmem_limit_bytes=VMEM_LIMIT),
        name=name)


def _rows(tm, width):
    return pl.BlockSpec((tm, width), lambda i, l: (i, 0))


def _layer(*tail):
    return pl.BlockSpec((None,) + tail, lambda *a: (a[-1][0],) + (0,) * len(tail))


def _rms(x, g):
    ms = jnp.mean(x * x, axis=-1, keepdims=True)
    return x * lax.rsqrt(ms + EPS) * g


def _gelu(x):
    c = math.sqrt(2.0 / math.pi)
    return 0.5 * x * (1.0 + jnp.tanh(c * (x + 0.044715 * (x * x * x))))


def _dot(a, b):
    return jnp.dot(a, b, preferred_element_type=F32)


def _dot_nt(a, b):
    return lax.dot_general(a, b, (((1,), (1,)), ((), ())), preferred_element_type=F32)


def _rope_kernel(pos_ref, fa_ref, ga_ref, fd_ref, gd_ref, ca_ref, sa_ref, cd_ref, sd_ref):
    pos = pos_ref[...].astype(F32)
    ang_a = pos * fa_ref[...]
    ca_ref[...] = jnp.cos(ang_a)
    sa_ref[...] = jnp.sin(ang_a) * ga_ref[...]
    ang_d = pos * fd_ref[...]
    cd_ref[...] = jnp.cos(ang_d)
    sd_ref[...] = jnp.sin(ang_d) * gd_ref[...]


def _rope_tables(pos, fa, ga, fd, gd):
    t = pos.shape[0]
    tm = 1024
    row = lambda w: pl.BlockSpec((tm, w), lambda i: (i, 0))
    vec = lambda w: pl.BlockSpec((1, w), lambda i: (0, 0))
    return pl.pallas_call(
        _rope_kernel,
        grid=(t // tm,),
        in_specs=[row(1), vec(LANES), vec(LANES), vec(LANES), vec(LANES)],
        out_specs=[row(LANES)] * 4,
        out_shape=[jax.ShapeDtypeStruct((t, LANES), F32)] * 4,
        compiler_params=pltpu.CompilerParams(dimension_semantics=("parallel",), vmem_limit_bytes=VMEM_LIMIT),
        name="rope_tables",
    )(pos, fa, ga, fd, gd)


def _mem_kv_kernel(mem_ref, g_ref, w_ref, o_ref, w16_ref):
    @pl.when(pl.program_id(1) == 0)
    def _():
        w16_ref[...] = w_ref[...].astype(BF16)

    hn = _rms(mem_ref[...], g_ref[...]).astype(BF16)
    o_ref[...] = _dot(hn, w16_ref[...]).astype(BF16)


def _mem_kv(mem2, g, w):
    rows = mem2.shape[0]
    tm = TM_KV
    return pl.pallas_call(
        _mem_kv_kernel,
        grid=(DEPTH, rows // tm),
        in_specs=[pl.BlockSpec((tm, D_MODEL), lambda l, i: (i, 0)),
                  pl.BlockSpec((None, 1, D_MODEL), lambda l, i: (l, 0, 0)),
                  pl.BlockSpec((None, D_MODEL, 2 * D_MODEL), lambda l, i: (l, 0, 0))],
        out_specs=pl.BlockSpec((None, tm, 2 * D_MODEL), lambda l, i: (l, i, 0)),
        out_shape=jax.ShapeDtypeStruct((DEPTH, rows, 2 * D_MODEL), BF16),
        scratch_shapes=[pltpu.VMEM((D_MODEL, 2 * D_MODEL), BF16)],
        compiler_params=pltpu.CompilerParams(dimension_semantics=("arbitrary", "arbitrary"),
                                             vmem_limit_bytes=VMEM_LIMIT),
        name="mem_kv",
    )(mem2, g, w)


def _mix_in_kernel(_, x_ref, g_ref, win_ref, gcq_ref, gckv_ref, wuq_ref, wuk_ref, wuv_ref,
                   ca_ref, sa_ref, cd_ref, sd_ref, gsgu_ref, ws_ref, bs_ref,
                   qa_ref, ka_ref, va_ref, sgu_ref, qd_ref, kd_ref, vd_ref):
    tm = x_ref.shape[0]
    part = tm // IN_PARTS
    parts = [pl.ds(r, part) for r in range(0, tm, part)]
    projs = [_dot(_rms(x_ref[r, :], g_ref[...]).astype(BF16), win_ref[...]) for r in parts]
    for r, proj in zip(parts, projs):
        _mix_in_part(r, proj, gcq_ref, gckv_ref, wuq_ref, wuk_ref, wuv_ref, ca_ref, sa_ref, cd_ref, sd_ref,
                     gsgu_ref, ws_ref, bs_ref, qa_ref, ka_ref, va_ref, sgu_ref, qd_ref, kd_ref, vd_ref)


def _mix_in_part(r, proj, gcq_ref, gckv_ref, wuq_ref, wuk_ref, wuv_ref, ca_ref, sa_ref, cd_ref, sd_ref,
                 gsgu_ref, ws_ref, bs_ref, qa_ref, ka_ref, va_ref, sgu_ref, qd_ref, kd_ref, vd_ref):
    tm = proj.shape[0]
    lane = lax.broadcasted_iota(jnp.int32, (tm, LANES), 1)
    low_half = lane < 64

    ca = ca_ref[r, :]
    sa = sa_ref[r, :]
    cqn = _rms(proj[:, O_CQ:O_CQ + MLA_Q_LORA], gcq_ref[...]).astype(BF16)
    q = _dot(cqn, wuq_ref[...])
    ckvn = _rms(proj[:, O_CKV:O_CKV + MLA_KV_LORA], gckv_ref[...]).astype(BF16)
    kn = _dot(ckvn, wuk_ref[...])
    vv = _dot(ckvn, wuv_ref[...])
    kr = proj[:, O_KR:O_KR + HEAD_PAD]
    kr = kr * ca + pltpu.roll(kr, 64, 1) * sa
    vv_t = vv.T
    ones_row = (lax.broadcasted_iota(jnp.int32, (VT_ROWS - MLA_V, tm), 0) == 0).astype(BF16)
    for hd in range(MLA_HEADS):
        qh = q[:, HEAD_PAD * hd:HEAD_PAD * (hd + 1)]
        qa_ref[hd, r, :] = ((qh * ca + pltpu.roll(qh, 64, 1) * sa) * Q_SCALE_MLA).astype(BF16)
        ka_ref[hd, r, :] = (kn[:, HEAD_PAD * hd:HEAD_PAD * (hd + 1)] + kr).astype(BF16)
        va_ref[hd, 0:MLA_V, r] = vv_t[MLA_V * hd:MLA_V * (hd + 1), :].astype(BF16)
        va_ref[hd, MLA_V:VT_ROWS, r] = ones_row

    zg = _gelu(proj[:, O_Z:O_Z + 2 * SGU_WIDTH])
    u = zg[:, :SGU_WIDTH]
    gsgu = gsgu_ref[...]
    inv_gd = 1.0 / SGU_GROUP_DIM
    vn_tiles = []
    for t in range(SGU_WIDTH // LANES):
        vt = zg[:, SGU_WIDTH + LANES * t:SGU_WIDTH + LANES * (t + 1)]
        s_lo = jnp.sum(jnp.where(low_half, vt, 0.0), axis=1, keepdims=True)
        s_hi = jnp.sum(jnp.where(low_half, 0.0, vt), axis=1, keepdims=True)
        xc = vt - jnp.where(low_half, s_lo, s_hi) * inv_gd
        sq = xc * xc
        v_lo = jnp.sum(jnp.where(low_half, sq, 0.0), axis=1, keepdims=True)
        v_hi = jnp.sum(jnp.where(low_half, 0.0, sq), axis=1, keepdims=True)
        var = jnp.where(low_half, v_lo, v_hi) * inv_gd
        vn_tiles.append((xc * lax.rsqrt(var + EPS) * gsgu[:, LANES * t:LANES * (t + 1)]).astype(BF16))
    bias = bs_ref[...]
    low_chunk = lax.broadcasted_iota(jnp.int32, (SGU_CHUNK, LANES), 1) < 64
    for n in range(tm // SGU_CHUNK):
        rows = slice(SGU_CHUNK * n, SGU_CHUNK * (n + 1))
        out_rows = pl.ds(r.start + SGU_CHUNK * n, SGU_CHUNK)
        for t in range(SGU_WIDTH // LANES):
            vc = vn_tiles[t][rows, :]
            m_lo = _dot(ws_ref[2 * t], vc)
            m_hi = _dot(ws_ref[2 * t + 1], vc)
            mixed = jnp.where(low_chunk, m_lo, m_hi) + bias[:, LANES * t:LANES * (t + 1)]
            sgu_ref[out_rows, LANES * t:LANES * (t + 1)] = (u[rows, LANES * t:LANES * (t + 1)] * mixed).astype(BF16)

    cd = cd_ref[r, :]
    sd = sd_ref[r, :]
    group = (lane & 31) >> 3
    for t in range(2):
        dq = proj[:, O_DQ + LANES * t:O_DQ + LANES * (t + 1)]
        qd_ref[r, LANES * t:LANES * (t + 1)] = (
            (dq * cd + pltpu.roll(dq, 64, 1) * sd) * Q_SCALE_DIFF).astype(BF16)
        dk = proj[:, O_DK + LANES * t:O_DK + LANES * (t + 1)]
        dk = dk * cd + pltpu.roll(dk, 64, 1) * sd
        for gl in range(DIFF_GROUPS // 2):
            kd_ref[4 * t + gl, r, :] = jnp.where(group == gl, dk, 0.0).astype(BF16)
    dv_t = proj[:, O_DV:O_DV + 256].T
    for hd in range(DIFF_HEADS):
        vd_ref[hd, 0:DIFF_V_DIM, r] = dv_t[DIFF_V_DIM * hd:DIFF_V_DIM * (hd + 1), :].astype(BF16)
        vd_ref[hd, DIFF_V_DIM:VT_ROWS, r] = ones_row


def _mix_in(lidx, x, p, ca, sa, cd, sd):
    t = x.shape[0]
    tm = TM_IN
    heads = lambda n, w: pl.BlockSpec((n, tm, w), lambda i, l: (0, i, 0))
    return _call(
        _mix_in_kernel, "mix_in", (t // tm,),
        in_specs=[_rows(tm, D_MODEL), _layer(1, D_MODEL), _layer(D_MODEL, IN_PAD),
                  _layer(1, MLA_Q_LORA), _layer(1, MLA_KV_LORA),
                  _layer(MLA_Q_LORA, MLA_HEADS * HEAD_PAD), _layer(MLA_KV_LORA, MLA_HEADS * HEAD_PAD),
                  _layer(MLA_KV_LORA, MLA_HEADS * MLA_V),
                  _rows(tm, LANES), _rows(tm, LANES), _rows(tm, LANES), _rows(tm, LANES),
                  _layer(1, SGU_WIDTH), _layer(SGU_GROUPS, SGU_CHUNK, SGU_CHUNK), _layer(SGU_CHUNK, SGU_WIDTH)],
        out_specs=[heads(MLA_HEADS, HEAD_PAD), heads(MLA_HEADS, HEAD_PAD),
                   pl.BlockSpec((MLA_HEADS, VT_ROWS, tm), lambda i, l: (0, 0, i)),
                   _rows(tm, SGU_WIDTH), _rows(tm, 256), heads(DIFF_GROUPS, LANES),
                   pl.BlockSpec((DIFF_HEADS, VT_ROWS, tm), lambda i, l: (0, 0, i))],
        out_shape=[jax.ShapeDtypeStruct((MLA_HEADS, t, HEAD_PAD), BF16),
                   jax.ShapeDtypeStruct((MLA_HEADS, t, HEAD_PAD), BF16),
                   jax.ShapeDtypeStruct((MLA_HEADS, VT_ROWS, t), BF16),
                   jax.ShapeDtypeStruct((t, SGU_WIDTH), BF16),
                   jax.ShapeDtypeStruct((t, 256), BF16),
                   jax.ShapeDtypeStruct((DIFF_GROUPS, t, LANES), BF16),
                   jax.ShapeDtypeStruct((DIFF_HEADS, VT_ROWS, t), BF16)],
        semantics=("parallel",),
    )(lidx, x, p["g_mix_pre"], p["w_in"], p["g_cq"], p["g_ckv"], p["w_uq"], p["w_uk"], p["w_uv"],
      ca, sa, cd, sd, p["g_sgu"], p["w_s"], p["b_s"])


def _attend_t(maps, seq, dv):
    n_chunks = seq // KV_CHUNK
    items = [(i, c) for i in range(len(maps)) for c in range(n_chunks)]
    scores = lambda item: _dot_nt(maps[item[0]][0](item[1]), maps[item[0]][1])
    pending = [scores(item) for item in items[:ATTN_AHEAD]]
    outs = []
    m = acc = None
    for n, (i, c) in enumerate(items):
        if n + ATTN_AHEAD < len(items):
            pending.append(scores(items[n + ATTN_AHEAD]))
        s = pending.pop(0)
        mc = jnp.max(s, axis=0, keepdims=True)
        m_new = mc if c == 0 else jnp.maximum(m, mc)
        e = jnp.exp2(s - m_new).astype(BF16)
        r = _dot(maps[i][2](c), e)
        acc = r if c == 0 else acc * jnp.exp2(m - m_new) + r
        m = m_new
        if c == n_chunks - 1:
            outs.append(acc[0:dv] / acc[dv:dv + 1])
    return outs


def _chunk(c):
    return pl.ds(c * KV_CHUNK, KV_CHUNK)


def _mla_attn_kernel(_, q_ref, k_ref, v_ref, o_ref):
    seq = k_ref.shape[1]
    maps = [(lambda c, hd=hd: k_ref[hd, _chunk(c), :], q_ref[hd], lambda c, hd=hd: v_ref[hd, :, _chunk(c)])
            for hd in range(MLA_HEADS)]
    outs = _attend_t(maps, seq, MLA_V)
    for pair in range(MLA_HEADS // 2):
        o_t = jnp.concatenate(outs[2 * pair:2 * pair + 2], axis=0)
        o_ref[:, LANES * pair:LANES * (pair + 1)] = o_t.T.astype(BF16)


def _mla_attn(lidx, qa, ka, va, batch, seq):
    t = qa.shape[1]
    nq = seq // TQ
    return _call(
        _mla_attn_kernel, "mla_attn", (batch, nq),
        in_specs=[pl.BlockSpec((MLA_HEADS, TQ, HEAD_PAD), lambda b, i, l: (0, b * nq + i, 0)),
                  pl.BlockSpec((MLA_HEADS, seq, HEAD_PAD), lambda b, i, l: (0, b, 0)),
                  pl.BlockSpec((MLA_HEADS, VT_ROWS, seq), lambda b, i, l: (0, 0, b))],
        out_specs=pl.BlockSpec((TQ, MLA_HEADS * MLA_V), lambda b, i, l: (b * nq + i, 0)),
        out_shape=jax.ShapeDtypeStruct((t, MLA_HEADS * MLA_V), BF16),
        semantics=("parallel", "arbitrary"),
    )(lidx, qa, ka, va)


def _diff_attn_kernel(_, lam_ref, linit_ref, gsub_ref, q_ref, k_ref, v_ref, o_ref):
    lam_init = linit_ref[...]
    lam = (jnp.exp(jnp.sum(lam_ref[0:1, :] * lam_ref[1:2, :], axis=1, keepdims=True))
           - jnp.exp(jnp.sum(lam_ref[2:3, :] * lam_ref[3:4, :], axis=1, keepdims=True)) + lam_init)
    gsub = gsub_ref[...] * (1.0 - lam_init)
    gsub = jnp.concatenate([gsub] * (TQ // LANES), axis=1)
    seq = k_ref.shape[1]
    q = q_ref[...]
    maps = [(lambda c, g=g: k_ref[g, _chunk(c), :], q[:, LANES * (g // 4):LANES * (g // 4 + 1)],
             lambda c, g=g: v_ref[g // 2, :, _chunk(c)]) for g in range(DIFF_GROUPS)]
    outs = _attend_t(maps, seq, DIFF_V_DIM)
    normed = []
    for hd in range(DIFF_HEADS):
        o = outs[2 * hd] - lam * outs[2 * hd + 1]
        ms = jnp.mean(o * o, axis=0, keepdims=True)
        normed.append(o * lax.rsqrt(ms + EPS) * gsub)
    for pair in range(DIFF_HEADS // 2):
        o_t = jnp.concatenate(normed[2 * pair:2 * pair + 2], axis=0)
        o_ref[:, LANES * pair:LANES * (pair + 1)] = o_t.T.astype(BF16)


def _diff_attn(lidx, p, qd, kd, vd, batch, seq):
    t = qd.shape[0]
    nq = seq // TQ
    return _call(
        _diff_attn_kernel, "diff_attn", (batch, nq),
        in_specs=[_layer(4, DIFF_QK_DIM), _layer(1, 1), _layer(DIFF_V_DIM, LANES),
                  pl.BlockSpec((TQ, 256), lambda b, i, l: (b * nq + i, 0)),
                  pl.BlockSpec((DIFF_GROUPS, seq, LANES), lambda b, i, l: (0, b, 0)),
                  pl.BlockSpec((DIFF_HEADS, VT_ROWS, seq), lambda b, i, l: (0, 0, b))],
        out_specs=pl.BlockSpec((TQ, 256), lambda b, i, l: (b * nq + i, 0)),
        out_shape=jax.ShapeDtypeStruct((t, DIFF_HEADS * DIFF_V_DIM), BF16),
        semantics=("parallel", "arbitrary"),
    )(lidx, p["lam"], p["lam_init"], p["g_sub"], qd, kd, vd)


def _mix_mem_kernel(_, x_ref, oa_ref, ob_ref, oc_ref, wmix_ref, gmix_ref, gpre_ref, wq_ref,
                    k_ref, v_ref, wo_ref, gpost_ref, o_ref):
    cat = jnp.concatenate([oa_ref[...], ob_ref[...], oc_ref[...]], axis=1)
    x1 = x_ref[...] + _rms(_dot(cat, wmix_ref[...]), gmix_ref[...])
    hq = _rms(x1, gpre_ref[...]).astype(BF16)
    q = _dot(hq, wq_ref[...]).astype(BF16)
    c = (X_HEAD_DIM ** -0.5) * LOG2E
    heads = []
    for hd in range(X_HEADS):
        cols = slice(X_HEAD_DIM * hd, X_HEAD_DIM * (hd + 1))
        s = _dot_nt(q[:, cols], k_ref[:, cols])
        m = jnp.max(s, axis=1, keepdims=True)
        e = jnp.exp2((s - m) * c)
        den = jnp.sum(e, axis=1, keepdims=True)
        heads.append((_dot(e.astype(BF16), v_ref[:, cols]) / den).astype(BF16))
    o = _dot(jnp.concatenate(heads, axis=1), wo_ref[...])
    o_ref[...] = x1 + _rms(o, gpost_ref[...])


def _mix_mem(lidx, x, oa, ob, oc, p, kv, seq):
    t = x.shape[0]
    tm = TM_MEM
    per_b = seq // tm
    return _call(
        _mix_mem_kernel, "mix_mem", (t // tm,),
        in_specs=[_rows(tm, D_MODEL), _rows(tm, MLA_HEADS * MLA_V), _rows(tm, SGU_WIDTH),
                  _rows(tm, DIFF_HEADS * DIFF_V_DIM),
                  _layer(D_MODEL, D_MODEL), _layer(1, D_MODEL), _layer(1, D_MODEL), _layer(D_MODEL, D_MODEL),
                  pl.BlockSpec((None, N_MEM, D_MODEL), lambda i, l: (l[0], i // per_b, 0)),
                  pl.BlockSpec((None, N_MEM, D_MODEL), lambda i, l: (l[0], i // per_b, 1)),
                  _layer(D_MODEL, D_MODEL), _layer(1, D_MODEL)],
        out_specs=_rows(tm, D_MODEL),
        out_shape=jax.ShapeDtypeStruct((t, D_MODEL), F32),
        semantics=("parallel",),
    )(lidx, x, oa, ob, oc, p["w_mix"], p["g_mix_post"], p["g_mem_pre"], p["w_mq"], kv, kv,
      p["w_mo"], p["g_mem_post"])


def _ffn_up_kernel(_, x_ref, xp_ref, xn_ref, g_ref, w_ref, cw_ref, cb_ref, o_ref, h_ref, *, seq):
    tm = x_ref.shape[0]
    i = pl.program_id(0)
    g = g_ref[...]
    at_start = (i * tm) % seq == 0
    at_end = ((i + 1) * tm) % seq == 0
    h_ref[0:FFN_HALO, :] = jnp.where(at_start, 0.0, _rms(xp_ref[...], g)).astype(BF16)
    h_ref[FFN_HALO:FFN_HALO + tm, :] = _rms(x_ref[...], g).astype(BF16)
    h_ref[FFN_HALO + tm:, :] = jnp.where(at_end, 0.0, _rms(xn_ref[...], g)).astype(BF16)
    hb = h_ref[...]
    rows = tm + 2 * FFN_HALO

    def project(cols):
        up_cols = slice(D_FF + cols.start, D_FF + cols.stop)
        return _dot(hb, w_ref[:, cols]), _dot(hb, w_ref[:, up_cols])

    def conv(a, cols):
        c = (cw_ref[0:1, cols] * pltpu.roll(a, 1, 0) + cw_ref[1:2, cols] * a
             + cw_ref[2:3, cols] * pltpu.roll(a, rows - 1, 0))
        return c[FFN_HALO:FFN_HALO + tm] + cb_ref[:, cols]

    blocks = [slice(lo, hi) for lo, hi in zip(FFN_SPLIT[:-1], FFN_SPLIT[1:])]
    acts = [project(cols) for cols in blocks]
    for cols, (a_gate, a_up) in zip(blocks, acts):
        up_cols = slice(D_FF + cols.start, D_FF + cols.stop)
        o_ref[:, cols] = (_gelu(conv(a_gate, cols)) * conv(a_up, up_cols)).astype(BF16)


def _ffn_up(lidx, x, p, seq):
    t = x.shape[0]
    tm = TM_FFN_UP
    hb = tm // FFN_HALO
    last = t // FFN_HALO - 1
    return _call(
        functools.partial(_ffn_up_kernel, seq=seq), "ffn_up", (t // tm,),
        in_specs=[_rows(tm, D_MODEL),
                  pl.BlockSpec((FFN_HALO, D_MODEL), lambda i, l: (jnp.maximum(i * hb - 1, 0), 0)),
                  pl.BlockSpec((FFN_HALO, D_MODEL), lambda i, l: (jnp.minimum((i + 1) * hb, last), 0)),
                  _layer(1, D_MODEL), _layer(D_MODEL, 2 * D_FF), _layer(CONV_WIDTH, 2 * D_FF), _layer(1, 2 * D_FF)],
        out_specs=_rows(tm, D_FF),
        out_shape=jax.ShapeDtypeStruct((t, D_FF), BF16),
        semantics=("parallel",),
        scratch=[pltpu.VMEM((tm + 2 * FFN_HALO, D_MODEL), BF16)],
    )(lidx, x, x, x, p["g_ffn_pre"], p["w_up"], p["conv_w"], p["conv_b"])


def _ffn_down_kernel(_, x_ref, a_ref, w_ref, g_ref, o_ref):
    o_ref[...] = x_ref[...] + _rms(_dot(a_ref[...], w_ref[...]), g_ref[...])


def _ffn_down(lidx, x, a, p):
    t = x.shape[0]
    tm = TM_FFN
    return _call(
        _ffn_down_kernel, "ffn_down", (t // tm,),
        in_specs=[_rows(tm, D_MODEL), _rows(tm, D_FF), _layer(D_FF, D_MODEL), _layer(1, D_MODEL)],
        out_specs=_rows(tm, D_MODEL),
        out_shape=jax.ShapeDtypeStruct((t, D_MODEL), F32),
        semantics=("parallel",),
    )(lidx, x, a, p["w_down"], p["g_ffn_post"])


def _take_cols(w, idx):
    cols = jnp.take(w, jnp.asarray(np.maximum(idx, 0)), axis=-1)
    return jnp.where(jnp.asarray(idx >= 0), cols, jnp.zeros((), w.dtype))


def _mla_head_lanes():
    src = np.full((HEAD_PAD,), -1, np.int64)
    src[0:16] = MLA_NOPE + np.arange(16)
    src[16:64] = np.arange(48)
    src[64:80] = MLA_NOPE + 16 + np.arange(16)
    src[80:96] = 48 + np.arange(16)
    return src


def _diff_lanes():
    src = np.zeros((256,), np.int64)
    for g in range(DIFF_GROUPS):
        base = g * DIFF_QK_DIM
        lane0 = LANES * (g // 4) + DIFF_ROT * (g % 4)
        for i in range(DIFF_ROT):
            src[lane0 + i] = base + i
            src[lane0 + 64 + i] = base + DIFF_ROT + i
            src[lane0 + 32 + i] = base + 2 * DIFF_ROT + i
            src[lane0 + 96 + i] = base + 3 * DIFF_ROT + i
    return src


def _layouts():
    head = _mla_head_lanes()
    p_cq, p_ckv, p_kr, p_sgu = MLA_Q_LORA, MLA_KV_LORA, MLA_ROPE, 2 * SGU_WIDTH
    o_kr = p_cq + p_ckv
    o_z = o_kr + p_kr
    o_dq = o_z + p_sgu
    o_dk = o_dq + 256
    o_dv = o_dk + 256
    kr_src = np.where(head >= MLA_NOPE, head - MLA_NOPE, -1)
    dl = _diff_lanes()

    def win_layout(w):
        out = jnp.concatenate([w[..., :o_kr], _take_cols(w[..., o_kr:o_z], kr_src), w[..., o_z:o_dq],
                               _take_cols(w[..., o_dq:o_dk], dl), _take_cols(w[..., o_dk:o_dv], dl),
                               w[..., o_dv:]], axis=-1)
        assert out.shape[-1] == IN_PAD
        return out
    per_q = MLA_NOPE + MLA_ROPE
    wuq_idx = np.concatenate([np.where(head >= 0, h * per_q + head, -1) for h in range(MLA_HEADS)])
    per_kv = MLA_NOPE + MLA_V
    k_src = np.where((head >= 0) & (head < MLA_NOPE), head, -1)
    wuk_idx = np.concatenate([np.where(k_src >= 0, h * per_kv + k_src, -1) for h in range(MLA_HEADS)])
    wuv_idx = np.concatenate([h * per_kv + MLA_NOPE + np.arange(MLA_V) for h in range(MLA_HEADS)])
    inv_a = ROPE_THETA ** (-jnp.arange(0, MLA_ROPE, 2, dtype=F32) / MLA_ROPE)
    inv_d = ROPE_THETA ** (-jnp.arange(0, 2 * DIFF_ROT, 2, dtype=F32) / (2 * DIFF_ROT))
    fa = jnp.zeros((LANES,), F32).at[0:16].set(inv_a).at[64:80].set(inv_a)
    ga = jnp.zeros((LANES,), F32).at[0:16].set(-1.0).at[64:80].set(1.0)
    fd = jnp.zeros((LANES,), F32).at[0:32].set(jnp.tile(inv_d, 4)).at[64:96].set(jnp.tile(inv_d, 4))
    gd = jnp.zeros((LANES,), F32).at[0:32].set(-1.0).at[64:96].set(1.0)
    return win_layout, wuq_idx, wuk_idx, wuv_idx, fa[None], ga[None], fd[None], gd[None]


def kernel(x, mem, positions, mix_pre_g, mix_post_g, w_in, mla_cq_g, mla_ckv_g, mla_w_uq, mla_w_ukv,
           sgu_norm_g, sgu_w_s, sgu_b_s, diff_lam_q1, diff_lam_k1, diff_lam_q2, diff_lam_k2, diff_sub_g,
           w_mix_out, mem_pre_g, mem_post_g, mem_kv_g, mem_w_q, mem_w_kv, mem_w_o,
           ffn_pre_g, ffn_post_g, ffn_w_up, ffn_conv_w, ffn_conv_b, ffn_w_down):
    batch, seq, d = x.shape
    t = batch * seq
    assert d == D_MODEL and seq % TM_FFN == 0 and seq % TQ == 0 and seq % SGU_CHUNK == 0
    win_layout, wuq_idx, wuk_idx, wuv_idx, fa, ga, fd, gd = _layouts()

    ca, sa, cd, sd = _rope_tables(positions.reshape(t, 1), fa, ga, fd, gd)

    vec = lambda a: a.astype(F32)[:, None, :]
    lam_init = np.array([0.8 - 0.6 * math.exp(-0.3 * l) for l in range(DEPTH)], np.float32)
    w_ukv16 = mla_w_ukv.astype(BF16)
    p = dict(
        g_mix_pre=vec(mix_pre_g), g_mix_post=vec(mix_post_g),
        w_in=win_layout(w_in.astype(BF16)),
        g_cq=vec(mla_cq_g), g_ckv=vec(mla_ckv_g),
        w_uq=_take_cols(mla_w_uq.astype(BF16), wuq_idx),
        w_uk=_take_cols(w_ukv16, wuk_idx),
        w_uv=_take_cols(w_ukv16, wuv_idx),
        g_sgu=vec(sgu_norm_g.reshape(DEPTH, SGU_WIDTH)),
        w_s=sgu_w_s.astype(BF16),
        b_s=jnp.repeat(jnp.swapaxes(sgu_b_s.astype(F32), 1, 2), SGU_GROUP_DIM, axis=2),
        lam=jnp.stack([diff_lam_q1, diff_lam_k1, diff_lam_q2, diff_lam_k2], axis=1).astype(F32),
        lam_init=jnp.asarray(lam_init).reshape(DEPTH, 1, 1),
        g_sub=jnp.repeat(diff_sub_g.astype(F32)[:, :, None], LANES, axis=2),
        w_mix=w_mix_out.astype(BF16),
        g_mem_pre=vec(mem_pre_g), g_mem_post=vec(mem_post_g),
        w_mq=mem_w_q.astype(BF16), w_mo=mem_w_o.astype(BF16),
        g_ffn_pre=vec(ffn_pre_g), g_ffn_post=vec(ffn_post_g),
        w_up=ffn_w_up.astype(BF16), conv_w=ffn_conv_w.astype(F32), conv_b=vec(ffn_conv_b),
        w_down=ffn_w_down.astype(BF16),
    )
    kv_all = _mem_kv(mem.reshape(batch * N_MEM, D_MODEL), vec(mem_kv_g), mem_w_kv)

    xc = x.reshape(t, D_MODEL)
    for l in range(DEPTH):
        lidx = jnp.full((1,), l, jnp.int32)
        qa, ka, va, sgu, qd, kd, vd = _mix_in(lidx, xc, p, ca, sa, cd, sd)
        out_a = _mla_attn(lidx, qa, ka, va, batch, seq)
        out_c = _diff_attn(lidx, p, qd, kd, vd, batch, seq)
        x2 = _mix_mem(lidx, xc, out_a, sgu, out_c, p, kv_all, seq)
        mid = _ffn_up(lidx, x2, p, seq)
        xc = _ffn_down(lidx, x2, mid, p)
    return xc.reshape(batch, seq, D_MODEL)
```

```python
import functools
import math

import jax
import jax.numpy as jnp
import numpy as np
from jax import lax
from jax.experimental import pallas as pl
from jax.experimental.pallas import tpu as pltpu

F32 = jnp.float32
BF16 = jnp.bfloat16

D_MODEL = 1024
DEPTH = 4
N_MEM = 256
ROPE_THETA = 500000.0
EPS = 1e-6

MLA_HEADS = 8
MLA_NOPE = 64
MLA_ROPE = 32
MLA_V = 64
MLA_Q_LORA = 384
MLA_KV_LORA = 256

SGU_GROUPS = 4
SGU_GROUP_DIM = 64
SGU_WIDTH = SGU_GROUPS * SGU_GROUP_DIM
SGU_CHUNK = 128

DIFF_HEADS = 4
DIFF_QK_DIM = 32
DIFF_V_DIM = 64
DIFF_ROT = 8
DIFF_GROUPS = 2 * DIFF_HEADS

X_HEADS = 4
X_HEAD_DIM = D_MODEL // X_HEADS

D_FF = 2816
CONV_WIDTH = 3

LANES = 128
HEAD_PAD = LANES
VT_ROWS = MLA_V + 16
LOG2E = math.log2(math.e)
Q_SCALE_MLA = (MLA_NOPE + MLA_ROPE) ** -0.5 * LOG2E
Q_SCALE_DIFF = DIFF_QK_DIM ** -0.5 * LOG2E
VMEM_LIMIT = 56 * 1024 * 1024

O_CQ = 0
O_CKV = O_CQ + MLA_Q_LORA
O_KR = O_CKV + MLA_KV_LORA
O_Z = O_KR + HEAD_PAD
O_DQ = O_Z + 2 * SGU_WIDTH
O_DK = O_DQ + 256
O_DV = O_DK + 256
IN_PAD = O_DV + 256

TM_IN = 512
IN_PARTS = 4
TQ = 256
KV_CHUNK = 256
ATTN_AHEAD = 5
TM_MEM = 1024
TM_KV = 1024
TM_FFN = 1024
TM_FFN_UP = 1024
FFN_HALO = 16
FFN_SPLIT = (0, 1536, D_FF)


def _call(body, name, grid, in_specs, out_specs, out_shape, semantics, scratch=()):
    return pl.pallas_call(
        body,
        grid_spec=pltpu.PrefetchScalarGridSpec(
            num_scalar_prefetch=1, grid=grid, in_specs=in_specs, out_specs=out_specs,
            scratch_shapes=list(scratch)),
        out_shape=out_shape,
        compiler_params=pltpu.CompilerParams(dimension_semantics=semantics, vmem_limit_bytes=VMEM_LIMIT),
        name=name)


def _rows(tm, width):
    return pl.BlockSpec((tm, width), lambda i, l: (i, 0))


def _layer(*tail):
    return pl.BlockSpec((None,) + tail, lambda *a: (a[-1][0],) + (0,) * len(tail))


def _rms(x, g):
    ms = jnp.mean(x * x, axis=-1, keepdims=True)
    return x * lax.rsqrt(ms + EPS) * g


def _gelu(x):
    c = math.sqrt(2.0 / math.pi)
    return 0.5 * x * (1.0 + jnp.tanh(c * (x + 0.044715 * (x * x * x))))


def _dot(a, b):
    return jnp.dot(a, b, preferred_element_type=F32)


def _dot_nt(a, b):
    return lax.dot_general(a, b, (((1,), (1,)), ((), ())), preferred_element_type=F32)


def _rope_kernel(pos_ref, fa_ref, ga_ref, fd_ref, gd_ref, ca_ref, sa_ref, cd_ref, sd_ref):
    pos = pos_ref[...].astype(F32)
    ang_a = pos * fa_ref[...]
    ca_ref[...] = jnp.cos(ang_a)
    sa_ref[...] = jnp.sin(ang_a) * ga_ref[...]
    ang_d = pos * fd_ref[...]
    cd_ref[...] = jnp.cos(ang_d)
    sd_ref[...] = jnp.sin(ang_d) * gd_ref[...]


def _rope_tables(pos, fa, ga, fd, gd):
    t = pos.shape[0]
    tm = 1024
    row = lambda w: pl.BlockSpec((tm, w), lambda i: (i, 0))
    vec = lambda w: pl.BlockSpec((1, w), lambda i: (0, 0))
    return pl.pallas_call(
        _rope_kernel,
        grid=(t // tm,),
        in_specs=[row(1), vec(LANES), vec(LANES), vec(LANES), vec(LANES)],
        out_specs=[row(LANES)] * 4,
        out_shape=[jax.ShapeDtypeStruct((t, LANES), F32)] * 4,
        compiler_params=pltpu.CompilerParams(dimension_semantics=("parallel",), vmem_limit_bytes=VMEM_LIMIT),
        name="rope_tables",
    )(pos, fa, ga, fd, gd)


def _mem_kv_kernel(mem_ref, g_ref, w_ref, o_ref, w16_ref):
    @pl.when(pl.program_id(1) == 0)
    def _():
        w16_ref[...] = w_ref[...].astype(BF16)

    hn = _rms(mem_ref[...], g_ref[...]).astype(BF16)
    o_ref[...] = _dot(hn, w16_ref[...]).astype(BF16)


def _mem_kv(mem2, g, w):
    rows = mem2.shape[0]
    tm = TM_KV
    return pl.pallas_call(
        _mem_kv_kernel,
        grid=(DEPTH, rows // tm),
        in_specs=[pl.BlockSpec((tm, D_MODEL), lambda l, i: (i, 0)),
                  pl.BlockSpec((None, 1, D_MODEL), lambda l, i: (l, 0, 0)),
                  pl.BlockSpec((None, D_MODEL, 2 * D_MODEL), lambda l, i: (l, 0, 0))],
        out_specs=pl.BlockSpec((None, tm, 2 * D_MODEL), lambda l, i: (l, i, 0)),
        out_shape=jax.ShapeDtypeStruct((DEPTH, rows, 2 * D_MODEL), BF16),
        scratch_shapes=[pltpu.VMEM((D_MODEL, 2 * D_MODEL), BF16)],
        compiler_params=pltpu.CompilerParams(dimension_semantics=("arbitrary", "arbitrary"),
                                             vmem_limit_bytes=VMEM_LIMIT),
        name="mem_kv",
    )(mem2, g, w)


def _mix_in_kernel(_, x_ref, g_ref, win_ref, gcq_ref, gckv_ref, wuq_ref, wuk_ref, wuv_ref,
                   ca_ref, sa_ref, cd_ref, sd_ref, gsgu_ref, ws_ref, bs_ref,
                   qa_ref, ka_ref, va_ref, sgu_ref, qd_ref, kd_ref, vd_ref):
    tm = x_ref.shape[0]
    part = tm // IN_PARTS
    parts = [pl.ds(r, part) for r in range(0, tm, part)]
    projs = [_dot(_rms(x_ref[r, :], g_ref[...]).astype(BF16), win_ref[...]) for r in parts]
    for r, proj in zip(parts, projs):
        _mix_in_part(r, proj, gcq_ref, gckv_ref, wuq_ref, wuk_ref, wuv_ref, ca_ref, sa_ref, cd_ref, sd_ref,
                     gsgu_ref, ws_ref, bs_ref, qa_ref, ka_ref, va_ref, sgu_ref, qd_ref, kd_ref, vd_ref)


def _mix_in_part(r, proj, gcq_ref, gckv_ref, wuq_ref, wuk_ref, wuv_ref, ca_ref, sa_ref, cd_ref, sd_ref,
                 gsgu_ref, ws_ref, bs_ref, qa_ref, ka_ref, va_ref, sgu_ref, qd_ref, kd_ref, vd_ref):
    tm = proj.shape[0]
    lane = lax.broadcasted_iota(jnp.int32, (tm, LANES), 1)
    low_half = lane < 64

    ca = ca_ref[r, :]
    sa = sa_ref[r, :]
    cqn = _rms(proj[:, O_CQ:O_CQ + MLA_Q_LORA], gcq_ref[...]).astype(BF16)
    q = _dot(cqn, wuq_ref[...])
    ckvn = _rms(proj[:, O_CKV:O_CKV + MLA_KV_LORA], gckv_ref[...]).astype(BF16)
    kn = _dot(ckvn, wuk_ref[...])
    vv = _dot(ckvn, wuv_ref[...])
    kr = proj[:, O_KR:O_KR + HEAD_PAD]
    kr = kr * ca + pltpu.roll(kr, 64, 1) * sa
    vv_t = vv.T
    ones_row = (lax.broadcasted_iota(jnp.int32, (VT_ROWS - MLA_V, tm), 0) == 0).astype(BF16)
    for hd in range(MLA_HEADS):
        qh = q[:, HEAD_PAD * hd:HEAD_PAD * (hd + 1)]
        qa_ref[hd, r, :] = ((qh * ca + pltpu.roll(qh, 64, 1) * sa) * Q_SCALE_MLA).astype(BF16)
        ka_ref[hd, r, :] = (kn[:, HEAD_PAD * hd:HEAD_PAD * (hd + 1)] + kr).astype(BF16)
        va_ref[hd, 0:MLA_V, r] = vv_t[MLA_V * hd:MLA_V * (hd + 1), :].astype(BF16)
        va_ref[hd, MLA_V:VT_ROWS, r] = ones_row

    zg = _gelu(proj[:, O_Z:O_Z + 2 * SGU_WIDTH])
    u = zg[:, :SGU_WIDTH]
    gsgu = gsgu_ref[...]
    inv_gd = 1.0 / SGU_GROUP_DIM
    vn_tiles = []
    for t in range(SGU_WIDTH // LANES):
        vt = zg[:, SGU_WIDTH + LANES * t:SGU_WIDTH + LANES * (t + 1)]
        s_lo = jnp.sum(jnp.where(low_half, vt, 0.0), axis=1, keepdims=True)
        s_hi = jnp.sum(jnp.where(low_half, 0.0, vt), axis=1, keepdims=True)
        xc = vt - jnp.where(low_half, s_lo, s_hi) * inv_gd
        sq = xc * xc
        v_lo = jnp.sum(jnp.where(low_half, sq, 0.0), axis=1, keepdims=True)
        v_hi = jnp.sum(jnp.where(low_half, 0.0, sq), axis=1, keepdims=True)
        var = jnp.where(low_half, v_lo, v_hi) * inv_gd
        vn_tiles.append((xc * lax.rsqrt(var + EPS) * gsgu[:, LANES * t:LANES * (t + 1)]).astype(BF16))
    bias = bs_ref[...]
    low_chunk = lax.broadcasted_iota(jnp.int32, (SGU_CHUNK, LANES), 1) < 64
    for n in range(tm // SGU_CHUNK):
        rows = slice(SGU_CHUNK * n, SGU_CHUNK * (n + 1))
        out_rows = pl.ds(r.start + SGU_CHUNK * n, SGU_CHUNK)
        for t in range(SGU_WIDTH // LANES):
            vc = vn_tiles[t][rows, :]
            m_lo = _dot(ws_ref[2 * t], vc)
            m_hi = _dot(ws_ref[2 * t + 1], vc)
            mixed = jnp.where(low_chunk, m_lo, m_hi) + bias[:, LANES * t:LANES * (t + 1)]
            sgu_ref[out_rows, LANES * t:LANES * (t + 1)] = (u[rows, LANES * t:LANES * (t + 1)] * mixed).astype(BF16)

    cd = cd_ref[r, :]
    sd = sd_ref[r, :]
    group = (lane & 31) >> 3
    for t in range(2):
        dq = proj[:, O_DQ + LANES * t:O_DQ + LANES * (t + 1)]
        qd_ref[r, LANES * t:LANES * (t + 1)] = (
            (dq * cd + pltpu.roll(dq, 64, 1) * sd) * Q_SCALE_DIFF).astype(BF16)
        dk = proj[:, O_DK + LANES * t:O_DK + LANES * (t + 1)]
        dk = dk * cd + pltpu.roll(dk, 64, 1) * sd
        for gl in range(DIFF_GROUPS // 2):
            kd_ref[4 * t + gl, r, :] = jnp.where(group == gl, dk, 0.0).astype(BF16)
    dv_t = proj[:, O_DV:O_DV + 256].T
    for hd in range(DIFF_HEADS):
        vd_ref[hd, 0:DIFF_V_DIM, r] = dv_t[DIFF_V_DIM * hd:DIFF_V_DIM * (hd + 1), :].astype(BF16)
        vd_ref[hd, DIFF_V_DIM:VT_ROWS, r] = ones_row


def _mix_in(lidx, x, p, ca, sa, cd, sd):
    t = x.shape[0]
    tm = TM_IN
    heads = lambda n, w: pl.BlockSpec((n, tm, w), lambda i, l: (0, i, 0))
    return _call(
        _mix_in_kernel, "mix_in", (t // tm,),
        in_specs=[_rows(tm, D_MODEL), _layer(1, D_MODEL), _layer(D_MODEL, IN_PAD),
                  _layer(1, MLA_Q_LORA), _layer(1, MLA_KV_LORA),
                  _layer(MLA_Q_LORA, MLA_HEADS * HEAD_PAD), _layer(MLA_KV_LORA, MLA_HEADS * HEAD_PAD),
                  _layer(MLA_KV_LORA, MLA_HEADS * MLA_V),
                  _rows(tm, LANES), _rows(tm, LANES), _rows(tm, LANES), _rows(tm, LANES),
                  _layer(1, SGU_WIDTH), _layer(SGU_GROUPS, SGU_CHUNK, SGU_CHUNK), _layer(SGU_CHUNK, SGU_WIDTH)],
        out_specs=[heads(MLA_HEADS, HEAD_PAD), heads(MLA_HEADS, HEAD_PAD),
                   pl.BlockSpec((MLA_HEADS, VT_ROWS, tm), lambda i, l: (0, 0, i)),
                   _rows(tm, SGU_WIDTH), _rows(tm, 256), heads(DIFF_GROUPS, LANES),
                   pl.BlockSpec((DIFF_HEADS, VT_ROWS, tm), lambda i, l: (0, 0, i))],
        out_shape=[jax.ShapeDtypeStruct((MLA_HEADS, t, HEAD_PAD), BF16),
                   jax.ShapeDtypeStruct((MLA_HEADS, t, HEAD_PAD), BF16),
                   jax.ShapeDtypeStruct((MLA_HEADS, VT_ROWS, t), BF16),
                   jax.ShapeDtypeStruct((t, SGU_WIDTH), BF16),
                   jax.ShapeDtypeStruct((t, 256), BF16),
                   jax.ShapeDtypeStruct((DIFF_GROUPS, t, LANES), BF16),
                   jax.ShapeDtypeStruct((DIFF_HEADS, VT_ROWS, t), BF16)],
        semantics=("parallel",),
    )(lidx, x, p["g_mix_pre"], p["w_in"], p["g_cq"], p["g_ckv"], p["w_uq"], p["w_uk"], p["w_uv"],
      ca, sa, cd, sd, p["g_sgu"], p["w_s"], p["b_s"])


def _attend_t(maps, seq, dv):
    n_chunks = seq // KV_CHUNK
    items = [(i, c) for i in range(len(maps)) for c in range(n_chunks)]
    scores = lambda item: _dot_nt(maps[item[0]][0](item[1]), maps[item[0]][1])
    pending = [scores(item) for item in items[:ATTN_AHEAD]]
    outs = []
    m = acc = None
    for n, (i, c) in enumerate(items):
        if n + ATTN_AHEAD < len(items):
            pending.append(scores(items[n + ATTN_AHEAD]))
        s = pending.pop(0)
        mc = jnp.max(s, axis=0, keepdims=True)
        m_new = mc if c == 0 else jnp.maximum(m, mc)
        e = jnp.exp2(s - m_new).astype(BF16)
        r = _dot(maps[i][2](c), e)
        acc = r if c == 0 else acc * jnp.exp2(m - m_new) + r
        m = m_new
        if c == n_chunks - 1:
            outs.append(acc[0:dv] / acc[dv:dv + 1])
    return outs


def _chunk(c):
    return pl.ds(c * KV_CHUNK, KV_CHUNK)


def _mla_attn_kernel(_, q_ref, k_ref, v_ref, o_ref):
    seq = k_ref.shape[1]
    maps = [(lambda c, hd=hd: k_ref[hd, _chunk(c), :], q_ref[hd], lambda c, hd=hd: v_ref[hd, :, _chunk(c)])
            for hd in range(MLA_HEADS)]
    outs = _attend_t(maps, seq, MLA_V)
    for pair in range(MLA_HEADS // 2):
        o_t = jnp.concatenate(outs[2 * pair:2 * pair + 2], axis=0)
        o_ref[:, LANES * pair:LANES * (pair + 1)] = o_t.T.astype(BF16)


def _mla_attn(lidx, qa, ka, va, batch, seq):
    t = qa.shape[1]
    nq = seq // TQ
    return _call(
        _mla_attn_kernel, "mla_attn", (batch, nq),
        in_specs=[pl.BlockSpec((MLA_HEADS, TQ, HEAD_PAD), lambda b, i, l: (0, b * nq + i, 0)),
                  pl.BlockSpec((MLA_HEADS, seq, HEAD_PAD), lambda b, i, l: (0, b, 0)),
                  pl.BlockSpec((MLA_HEADS, VT_ROWS, seq), lambda b, i, l: (0, 0, b))],
        out_specs=pl.BlockSpec((TQ, MLA_HEADS * MLA_V), lambda b, i, l: (b * nq + i, 0)),
        out_shape=jax.ShapeDtypeStruct((t, MLA_HEADS * MLA_V), BF16),
        semantics=("parallel", "arbitrary"),
    )(lidx, qa, ka, va)


def _diff_attn_kernel(_, lam_ref, linit_ref, gsub_ref, q_ref, k_ref, v_ref, o_ref):
    lam_init = linit_ref[...]
    lam = (jnp.exp(jnp.sum(lam_ref[0:1, :] * lam_ref[1:2, :], axis=1, keepdims=True))
           - jnp.exp(jnp.sum(lam_ref[2:3, :] * lam_ref[3:4, :], axis=1, keepdims=True)) + lam_init)
    gsub = gsub_ref[...] * (1.0 - lam_init)
    gsub = jnp.concatenate([gsub] * (TQ // LANES), axis=1)
    seq = k_ref.shape[1]
    q = q_ref[...]
    maps = [(lambda c, g=g: k_ref[g, _chunk(c), :], q[:, LANES * (g // 4):LANES * (g // 4 + 1)],
             lambda c, g=g: v_ref[g // 2, :, _chunk(c)]) for g in range(DIFF_GROUPS)]
    outs = _attend_t(maps, seq, DIFF_V_DIM)
    normed = []
    for hd in range(DIFF_HEADS):
        o = outs[2 * hd] - lam * outs[2 * hd + 1]
        ms = jnp.mean(o * o, axis=0, keepdims=True)
        normed.append(o * lax.rsqrt(ms + EPS) * gsub)
    for pair in range(DIFF_HEADS // 2):
        o_t = jnp.concatenate(normed[2 * pair:2 * pair + 2], axis=0)
        o_ref[:, LANES * pair:LANES * (pair + 1)] = o_t.T.astype(BF16)


def _diff_attn(lidx, p, qd, kd, vd, batch, seq):
    t = qd.shape[0]
    nq = seq // TQ
    return _call(
        _diff_attn_kernel, "diff_attn", (batch, nq),
        in_specs=[_layer(4, DIFF_QK_DIM), _layer(1, 1), _layer(DIFF_V_DIM, LANES),
                  pl.BlockSpec((TQ, 256), lambda b, i, l: (b * nq + i, 0)),
                  pl.BlockSpec((DIFF_GROUPS, seq, LANES), lambda b, i, l: (0, b, 0)),
                  pl.BlockSpec((DIFF_HEADS, VT_ROWS, seq), lambda b, i, l: (0, 0, b))],
        out_specs=pl.BlockSpec((TQ, 256), lambda b, i, l: (b * nq + i, 0)),
        out_shape=jax.ShapeDtypeStruct((t, DIFF_HEADS * DIFF_V_DIM), BF16),
        semantics=("parallel", "arbitrary"),
    )(lidx, p["lam"], p["lam_init"], p["g_sub"], qd, kd, vd)


def _mix_mem_kernel(_, x_ref, oa_ref, ob_ref, oc_ref, wmix_ref, gmix_ref, gpre_ref, wq_ref,
                    k_ref, v_ref, wo_ref, gpost_ref, o_ref):
    cat = jnp.concatenate([oa_ref[...], ob_ref[...], oc_ref[...]], axis=1)
    x1 = x_ref[...] + _rms(_dot(cat, wmix_ref[...]), gmix_ref[...])
    hq = _rms(x1, gpre_ref[...]).astype(BF16)
    q = _dot(hq, wq_ref[...]).astype(BF16)
    c = (X_HEAD_DIM ** -0.5) * LOG2E
    heads = []
    for hd in range(X_HEADS):
        cols = slice(X_HEAD_DIM * hd, X_HEAD_DIM * (hd + 1))
        s = _dot_nt(q[:, cols], k_ref[:, cols])
        m = jnp.max(s, axis=1, keepdims=True)
        e = jnp.exp2((s - m) * c)
        den = jnp.sum(e, axis=1, keepdims=True)
        heads.append((_dot(e.astype(BF16), v_ref[:, cols]) / den).astype(BF16))
    o = _dot(jnp.concatenate(heads, axis=1), wo_ref[...])
    o_ref[...] = x1 + _rms(o, gpost_ref[...])


def _mix_mem(lidx, x, oa, ob, oc, p, kv, seq):
    t = x.shape[0]
    tm = TM_MEM
    per_b = seq // tm
    return _call(
        _mix_mem_kernel, "mix_mem", (t // tm,),
        in_specs=[_rows(tm, D_MODEL), _rows(tm, MLA_HEADS * MLA_V), _rows(tm, SGU_WIDTH),
                  _rows(tm, DIFF_HEADS * DIFF_V_DIM),
                  _layer(D_MODEL, D_MODEL), _layer(1, D_MODEL), _layer(1, D_MODEL), _layer(D_MODEL, D_MODEL),
                  pl.BlockSpec((None, N_MEM, D_MODEL), lambda i, l: (l[0], i // per_b, 0)),
                  pl.BlockSpec((None, N_MEM, D_MODEL), lambda i, l: (l[0], i // per_b, 1)),
                  _layer(D_MODEL, D_MODEL), _layer(1, D_MODEL)],
        out_specs=_rows(tm, D_MODEL),
        out_shape=jax.ShapeDtypeStruct((t, D_MODEL), F32),
        semantics=("parallel",),
    )(lidx, x, oa, ob, oc, p["w_mix"], p["g_mix_post"], p["g_mem_pre"], p["w_mq"], kv, kv,
      p["w_mo"], p["g_mem_post"])


def _ffn_up_kernel(_, x_ref, xp_ref, xn_ref, g_ref, w_ref, cw_ref, cb_ref, o_ref, h_ref, *, seq):
    tm = x_ref.shape[0]
    i = pl.program_id(0)
    g = g_ref[...]
    at_start = (i * tm) % seq == 0
    at_end = ((i + 1) * tm) % seq == 0
    h_ref[0:FFN_HALO, :] = jnp.where(at_start, 0.0, _rms(xp_ref[...], g)).astype(BF16)
    h_ref[FFN_HALO:FFN_HALO + tm, :] = _rms(x_ref[...], g).astype(BF16)
    h_ref[FFN_HALO + tm:, :] = jnp.where(at_end, 0.0, _rms(xn_ref[...], g)).astype(BF16)
    hb = h_ref[...]
    rows = tm + 2 * FFN_HALO

    def project(cols):
        up_cols = slice(D_FF + cols.start, D_FF + cols.stop)
        return _dot(hb, w_ref[:, cols]), _dot(hb, w_ref[:, up_cols])

    def conv(a, cols):
        c = (cw_ref[0:1, cols] * pltpu.roll(a, 1, 0) + cw_ref[1:2, cols] * a
             + cw_ref[2:3, cols] * pltpu.roll(a, rows - 1, 0))
        return c[FFN_HALO:FFN_HALO + tm] + cb_ref[:, cols]

    blocks = [slice(lo, hi) for lo, hi in zip(FFN_SPLIT[:-1], FFN_SPLIT[1:])]
    acts = [project(cols) for cols in blocks]
    for cols, (a_gate, a_up) in zip(blocks, acts):
        up_cols = slice(D_FF + cols.start, D_FF + cols.stop)
        o_ref[:, cols] = (_gelu(conv(a_gate, cols)) * conv(a_up, up_cols)).astype(BF16)


def _ffn_up(lidx, x, p, seq):
    t = x.shape[0]
    tm = TM_FFN_UP
    hb = tm // FFN_HALO
    last = t // FFN_HALO - 1
    return _call(
        functools.partial(_ffn_up_kernel, seq=seq), "ffn_up", (t // tm,),
        in_specs=[_rows(tm, D_MODEL),
                  pl.BlockSpec((FFN_HALO, D_MODEL), lambda i, l: (jnp.maximum(i * hb - 1, 0), 0)),
                  pl.BlockSpec((FFN_HALO, D_MODEL), lambda i, l: (jnp.minimum((i + 1) * hb, last), 0)),
                  _layer(1, D_MODEL), _layer(D_MODEL, 2 * D_FF), _layer(CONV_WIDTH, 2 * D_FF), _layer(1, 2 * D_FF)],
        out_specs=_rows(tm, D_FF),
        out_shape=jax.ShapeDtypeStruct((t, D_FF), BF16),
        semantics=("parallel",),
        scratch=[pltpu.VMEM((tm + 2 * FFN_HALO, D_MODEL), BF16)],
    )(lidx, x, x, x, p["g_ffn_pre"], p["w_up"], p["conv_w"], p["conv_b"])


def _ffn_down_kernel(_, x_ref, a_ref, w_ref, g_ref, o_ref):
    o_ref[...] = x_ref[...] + _rms(_dot(a_ref[...], w_ref[...]), g_ref[...])


def _ffn_down(lidx, x, a, p):
    t = x.shape[0]
    tm = TM_FFN
    return _call(
        _ffn_down_kernel, "ffn_down", (t // tm,),
        in_specs=[_rows(tm, D_MODEL), _rows(tm, D_FF), _layer(D_FF, D_MODEL), _layer(1, D_MODEL)],
        out_specs=_rows(tm, D_MODEL),
        out_shape=jax.ShapeDtypeStruct((t, D_MODEL), F32),
        semantics=("parallel",),
    )(lidx, x, a, p["w_down"], p["g_ffn_post"])


def _take_cols(w, idx):
    cols = jnp.take(w, jnp.asarray(np.maximum(idx, 0)), axis=-1)
    return jnp.where(jnp.asarray(idx >= 0), cols, jnp.zeros((), w.dtype))


def _mla_head_lanes():
    src = np.full((HEAD_PAD,), -1, np.int64)
    src[0:16] = MLA_NOPE + np.arange(16)
    src[16:64] = np.arange(48)
    src[64:80] = MLA_NOPE + 16 + np.arange(16)
    src[80:96] = 48 + np.arange(16)
    return src


def _diff_lanes():
    src = np.zeros((256,), np.int64)
    for g in range(DIFF_GROUPS):
        base = g * DIFF_QK_DIM
        lane0 = LANES * (g // 4) + DIFF_ROT * (g % 4)
        for i in range(DIFF_ROT):
            src[lane0 + i] = base + i
            src[lane0 + 64 + i] = base + DIFF_ROT + i
            src[lane0 + 32 + i] = base + 2 * DIFF_ROT + i
            src[lane0 + 96 + i] = base + 3 * DIFF_ROT + i
    return src


def _layouts():
    head = _mla_head_lanes()
    p_cq, p_ckv, p_kr, p_sgu = MLA_Q_LORA, MLA_KV_LORA, MLA_ROPE, 2 * SGU_WIDTH
    o_kr = p_cq + p_ckv
    o_z = o_kr + p_kr
    o_dq = o_z + p_sgu
    o_dk = o_dq + 256
    o_dv = o_dk + 256
    kr_src = np.where(head >= MLA_NOPE, head - MLA_NOPE, -1)
    dl = _diff_lanes()

    def win_layout(w):
        out = jnp.concatenate([w[..., :o_kr], _take_cols(w[..., o_kr:o_z], kr_src), w[..., o_z:o_dq],
                               _take_cols(w[..., o_dq:o_dk], dl), _take_cols(w[..., o_dk:o_dv], dl),
                               w[..., o_dv:]], axis=-1)
        assert out.shape[-1] == IN_PAD
        return out
    per_q = MLA_NOPE + MLA_ROPE
    wuq_idx = np.concatenate([np.where(head >= 0, h * per_q + head, -1) for h in range(MLA_HEADS)])
    per_kv = MLA_NOPE + MLA_V
    k_src = np.where((head >= 0) & (head < MLA_NOPE), head, -1)
    wuk_idx = np.concatenate([np.where(k_src >= 0, h * per_kv + k_src, -1) for h in range(MLA_HEADS)])
    wuv_idx = np.concatenate([h * per_kv + MLA_NOPE + np.arange(MLA_V) for h in range(MLA_HEADS)])
    inv_a = ROPE_THETA ** (-jnp.arange(0, MLA_ROPE, 2, dtype=F32) / MLA_ROPE)
    inv_d = ROPE_THETA ** (-jnp.arange(0, 2 * DIFF_ROT, 2, dtype=F32) / (2 * DIFF_ROT))
    fa = jnp.zeros((LANES,), F32).at[0:16].set(inv_a).at[64:80].set(inv_a)
    ga = jnp.zeros((LANES,), F32).at[0:16].set(-1.0).at[64:80].set(1.0)
    fd = jnp.zeros((LANES,), F32).at[0:32].set(jnp.tile(inv_d, 4)).at[64:96].set(jnp.tile(inv_d, 4))
    gd = jnp.zeros((LANES,), F32).at[0:32].set(-1.0).at[64:96].set(1.0)
    return win_layout, wuq_idx, wuk_idx, wuv_idx, fa[None], ga[None], fd[None], gd[None]


def kernel(x, mem, positions, mix_pre_g, mix_post_g, w_in, mla_cq_g, mla_ckv_g, mla_w_uq, mla_w_ukv,
           sgu_norm_g, sgu_w_s, sgu_b_s, diff_lam_q1, diff_lam_k1, diff_lam_q2, diff_lam_k2, diff_sub_g,
           w_mix_out, mem_pre_g, mem_post_g, mem_kv_g, mem_w_q, mem_w_kv, mem_w_o,
           ffn_pre_g, ffn_post_g, ffn_w_up, ffn_conv_w, ffn_conv_b, ffn_w_down):
    batch, seq, d = x.shape
    t = batch * seq
    assert d == D_MODEL and seq % TM_FFN == 0 and seq % TQ == 0 and seq % SGU_CHUNK == 0
    win_layout, wuq_idx, wuk_idx, wuv_idx, fa, ga, fd, gd = _layouts()

    ca, sa, cd, sd = _rope_tables(positions.reshape(t, 1), fa, ga, fd, gd)

    vec = lambda a: a.astype(F32)[:, None, :]
    lam_init = np.array([0.8 - 0.6 * math.exp(-0.3 * l) for l in range(DEPTH)], np.float32)
    w_ukv16 = mla_w_ukv.astype(BF16)
    p = dict(
        g_mix_pre=vec(mix_pre_g), g_mix_post=vec(mix_post_g),
        w_in=win_layout(w_in.astype(BF16)),
        g_cq=vec(mla_cq_g), g_ckv=vec(mla_ckv_g),
        w_uq=_take_cols(mla_w_uq.astype(BF16), wuq_idx),
        w_uk=_take_cols(w_ukv16, wuk_idx),
        w_uv=_take_cols(w_ukv16, wuv_idx),
        g_sgu=vec(sgu_norm_g.reshape(DEPTH, SGU_WIDTH)),
        w_s=sgu_w_s.astype(BF16),
        b_s=jnp.repeat(jnp.swapaxes(sgu_b_s.astype(F32), 1, 2), SGU_GROUP_DIM, axis=2),
        lam=jnp.stack([diff_lam_q1, diff_lam_k1, diff_lam_q2, diff_lam_k2], axis=1).astype(F32),
        lam_init=jnp.asarray(lam_init).reshape(DEPTH, 1, 1),
        g_sub=jnp.repeat(diff_sub_g.astype(F32)[:, :, None], LANES, axis=2),
        w_mix=w_mix_out.astype(BF16),
        g_mem_pre=vec(mem_pre_g), g_mem_post=vec(mem_post_g),
        w_mq=mem_w_q.astype(BF16), w_mo=mem_w_o.astype(BF16),
        g_ffn_pre=vec(ffn_pre_g), g_ffn_post=vec(ffn_post_g),
        w_up=ffn_w_up.astype(BF16), conv_w=ffn_conv_w.astype(F32), conv_b=vec(ffn_conv_b),
        w_down=ffn_w_down.astype(BF16),
    )
    kv_all = _mem_kv(mem.reshape(batch * N_MEM, D_MODEL), vec(mem_kv_g), mem_w_kv)

    xc = x.reshape(t, D_MODEL)
    for l in range(DEPTH):
        lidx = jnp.full((1,), l, jnp.int32)
        qa, ka, va, sgu, qd, kd, vd = _mix_in(lidx, xc, p, ca, sa, cd, sd)
        out_a = _mla_attn(lidx, qa, ka, va, batch, seq)
        out_c = _diff_attn(lidx, p, qd, kd, vd, batch, seq)
        x2 = _mix_mem(lidx, xc, out_a, sgu, out_c, p, kv_all, seq)
        mid = _ffn_up(lidx, x2, p, seq)
        xc = _ffn_down(lidx, x2, mid, p)
    return xc.reshape(batch, seq, D_MODEL)
```
